```python
import jax, jax.numpy as jnp
from jax import lax
import numpy as np

D_MODEL = 2048
BATCH = 2
SEQ = 8192
DEPTH = 2

CHUNK = 64
Q_BLOCK = 128

MLA_HEADS = 8
MLA_Q_RANK = 512
MLA_KV_RANK = 512
MLA_NOPE = 128
MLA_ROPE = 64
MLA_V = 128
ROPE_THETA = 10000.0

GLA_HEADS = 4
GLA_DK = 128
GLA_DV = 256
GLA_GATE_RANK = 16
GLA_GATE_NORM = 16.0

SGU_BLOCK = 128
SGU_DFF = 2 * D_MODEL
SGU_HALF = SGU_DFF // 2
SGU_GROUPS = 16
SGU_GROUP_DIM = SGU_HALF // SGU_GROUPS

N_EXPERTS = 16
N_GROUPS = 4
EXPERTS_PER_GROUP = N_EXPERTS // N_GROUPS
GROUP_SCORE_K = 2
TOP_K = 2
D_EXPERT = 512

ALPHA = (2.0 * DEPTH) ** 0.25
BETA = (8.0 * DEPTH) ** -0.25
EPS = 1e-6

N_EVEN = (DEPTH + 1) // 2
N_ODD = DEPTH // 2

EVEN_WIDTHS = (MLA_Q_RANK, MLA_KV_RANK, MLA_ROPE,
               GLA_HEADS * GLA_DK, GLA_HEADS * GLA_DK, GLA_HEADS * GLA_DV,
               GLA_GATE_RANK, GLA_HEADS * GLA_DV)
EVEN_SPLITS = tuple(int(c) for c in np.cumsum(EVEN_WIDTHS)[:-1])
EVEN_IN = int(sum(EVEN_WIDTHS))
MIX_WIDTH = MLA_HEADS * MLA_V + GLA_HEADS * GLA_DV

kernel_name = "hybrid_mla_gla_gmlp_groupmoe_deepnorm"


def layer_norm(x, g, b):
    xf = x.astype(jnp.float32)
    mu = jnp.mean(xf, -1, keepdims=True)
    var = jnp.mean(jnp.square(xf - mu), -1, keepdims=True)
    return ((xf - mu) * lax.rsqrt(var + EPS) * g + b).astype(x.dtype)


def rms_norm(x, g):
    xf = x.astype(jnp.float32)
    return (xf * lax.rsqrt(jnp.mean(jnp.square(xf), -1, keepdims=True) + EPS) * g).astype(x.dtype)


def rope_tables(positions):
    inv = 1.0 / (ROPE_THETA ** (jnp.arange(0, MLA_ROPE, 2, dtype=jnp.float32) / MLA_ROPE))
    ang = positions.astype(jnp.float32)[..., None] * inv
    return jnp.cos(ang), jnp.sin(ang)


def apply_rope(x, cos, sin):
    x1, x2 = jnp.split(x.astype(jnp.float32), 2, axis=-1)
    return jnp.concatenate([x1 * cos - x2 * sin, x2 * cos + x1 * sin], -1).astype(x.dtype)


def mla(c_q, c_kv, k_rope, g_q, g_kv, w_uq, w_ukv, cos, sin):
    B, S, _ = c_q.shape
    q = jnp.einsum('bsr,rhd->bshd', rms_norm(c_q, g_q), w_uq)
    q_nope = q[..., :MLA_NOPE]
    q_rot = apply_rope(q[..., MLA_NOPE:], cos[:, :, None], sin[:, :, None])
    kv = jnp.einsum('bsr,rhd->bshd', rms_norm(c_kv, g_kv), w_ukv)
    k_nope, v = kv[..., :MLA_NOPE], kv[..., MLA_NOPE:]
    k_rot = apply_rope(k_rope, cos, sin)
    scale = (MLA_NOPE + MLA_ROPE) ** -0.5
    n_blk = S // Q_BLOCK
    qn = q_nope.reshape(B, n_blk, Q_BLOCK, MLA_HEADS, MLA_NOPE).transpose(1, 0, 2, 3, 4)
    qr = q_rot.reshape(B, n_blk, Q_BLOCK, MLA_HEADS, MLA_ROPE).transpose(1, 0, 2, 3, 4)
    key_chunk = jnp.arange(S) // CHUNK

    def block(args):
        i, qn_b, qr_b = args
        s = (jnp.einsum('bqhd,bkhd->bhqk', qn_b, k_nope)
             + jnp.einsum('bqhd,bkd->bhqk', qr_b, k_rot)).astype(jnp.float32) * scale
        q_chunk = (i * Q_BLOCK + jnp.arange(Q_BLOCK)) // CHUNK
        mask = key_chunk[None, :] <= q_chunk[:, None]
        p = jax.nn.softmax(jnp.where(mask, s, -jnp.inf), axis=-1).astype(v.dtype)
        return jnp.einsum('bhqk,bkhd->bqhd', p, v)

    o = lax.map(block, (jnp.arange(n_blk), qn, qr))
    return o.transpose(1, 0, 2, 3, 4).reshape(B, S, MLA_HEADS * MLA_V)


def gla(q, k, v, gate_lr, g_out, w_gk2, b_gk2, g_norm):
    B, S, _ = q.shape
    nc = S // CHUNK
    log_g = jax.nn.log_sigmoid((gate_lr @ w_gk2 + b_gk2).astype(jnp.float32)) / GLA_GATE_NORM
    qh = q.astype(jnp.float32).reshape(B, nc, CHUNK, GLA_HEADS, GLA_DK) * GLA_DK ** -0.5
    kh = k.astype(jnp.float32).reshape(B, nc, CHUNK, GLA_HEADS, GLA_DK)
    vh = v.astype(jnp.float32).reshape(B, nc, CHUNK, GLA_HEADS, GLA_DV)
    cum = jnp.cumsum(log_g.reshape(B, nc, CHUNK, GLA_HEADS, GLA_DK), axis=2)
    cum_end = cum[:, :, -1]
    k_dec = kh * jnp.exp(cum_end[:, :, None] - cum)
    upd = jnp.einsum('bclhk,bclhv->bchkv', k_dec, vh)
    decay = jnp.exp(cum_end)

    def step(state, inp):
        d, u = inp
        state = d[..., None] * state + u
        return state, state

    init = jnp.zeros((B, GLA_HEADS, GLA_DK, GLA_DV), jnp.float32)
    _, states = lax.scan(step, init, (decay.transpose(1, 0, 2, 3), upd.transpose(1, 0, 2, 3, 4)))
    o = jnp.einsum('bclhk,cbhkv->bclhv', qh, states)
    o = o.reshape(B, S, GLA_HEADS, GLA_DV).astype(q.dtype)
    o = rms_norm(o, g_norm) * jax.nn.silu(g_out.reshape(B, S, GLA_HEADS, GLA_DV))
    return o.reshape(B, S, GLA_HEADS * GLA_DV)


def spatial_gating(h, ln_g, ln_b, w_s, b_s):
    B, S, _ = h.shape
    u, v = h[..., :SGU_HALF], h[..., SGU_HALF:]
    v = layer_norm(v, ln_g, ln_b)
    nb = S // SGU_BLOCK
    vb = v.reshape(B, nb, SGU_BLOCK, SGU_GROUPS, SGU_GROUP_DIM)
    pos_chunk = jnp.arange(SGU_BLOCK) // CHUNK
    mask = pos_chunk[None, :] <= pos_chunk[:, None]
    w = jnp.where(mask[None], w_s, jnp.zeros_like(w_s))
    mixed = jnp.einsum('gij,bnjgc->bnigc', w, vb) + b_s.T[None, None, :, :, None]
    return u * mixed.reshape(B, S, SGU_HALF)


def grouped_moe(x, router_w, router_b, w_gate, w_up, w_down):
    B, S, D = x.shape
    t = x.reshape(B * S, D)
    scores = jax.nn.sigmoid((t @ router_w).astype(jnp.float32))
    biased = scores + router_b.astype(jnp.float32)
    grp = biased.reshape(-1, N_GROUPS, EXPERTS_PER_GROUP)
    grp_score = lax.top_k(grp, GROUP_SCORE_K)[0].sum(-1)
    best = jnp.argmax(grp_score, axis=-1)
    in_grp = jnp.repeat(jax.nn.one_hot(best, N_GROUPS, dtype=jnp.bool_), EXPERTS_PER_GROUP, axis=-1)
    _, idx = lax.top_k(jnp.where(in_grp, biased, -jnp.inf), TOP_K)
    gate = jnp.take_along_axis(scores, idx, axis=1)
    gate = gate / jnp.sum(gate, -1, keepdims=True)
    combine = jnp.einsum('tk,tke->te', gate, jax.nn.one_hot(idx, N_EXPERTS, dtype=jnp.float32)).astype(t.dtype)
    y = jnp.zeros_like(t)
    for e in range(N_EXPERTS):
        h = jax.nn.silu(t @ w_gate[e]) * (t @ w_up[e])
        y = y + combine[:, e:e + 1] * (h @ w_down[e])
    return y.reshape(B, S, D)


def setup_inputs(seed: int = 0) -> dict:
    key = jax.random.key(seed)
    ks = iter(jax.random.split(key, 32))
    nrm = lambda shape, s: jax.random.normal(next(ks), shape, jnp.float32) * s
    gain = lambda shape: 1.0 + nrm(shape, 0.02)
    offset = jax.random.randint(next(ks), (BATCH, 1), 0, 1024, jnp.int32) * CHUNK
    return {
        "x": nrm((BATCH, SEQ, D_MODEL), 1.0),
        "positions": offset + jnp.arange(SEQ, dtype=jnp.int32)[None, :],
        "ev_w_in": nrm((N_EVEN, D_MODEL, EVEN_IN), D_MODEL ** -0.5),
        "ev_q_norm": gain((N_EVEN, MLA_Q_RANK)),
        "ev_kv_norm": gain((N_EVEN, MLA_KV_RANK)),
        "ev_w_uq": nrm((N_EVEN, MLA_Q_RANK, MLA_HEADS, MLA_NOPE + MLA_ROPE), MLA_Q_RANK ** -0.5),
        "ev_w_ukv": nrm((N_EVEN, MLA_KV_RANK, MLA_HEADS, MLA_NOPE + MLA_V), MLA_KV_RANK ** -0.5),
        "ev_w_gk2": nrm((N_EVEN, GLA_GATE_RANK, GLA_HEADS * GLA_DK), GLA_GATE_RANK ** -0.5),
        "ev_b_gk2": nrm((N_EVEN, GLA_HEADS * GLA_DK), 0.1),
        "ev_gla_norm": gain((N_EVEN, GLA_DV)),
        "ev_w_out": nrm((N_EVEN, MIX_WIDTH, D_MODEL), MIX_WIDTH ** -0.5 * BETA),
        "od_w_in": nrm((N_ODD, D_MODEL, SGU_DFF), D_MODEL ** -0.5),
        "od_sgu_ln_g": gain((N_ODD, SGU_HALF)),
        "od_sgu_ln_b": nrm((N_ODD, SGU_HALF), 0.02),
        "od_w_s": nrm((N_ODD, SGU_GROUPS, SGU_BLOCK, SGU_BLOCK), SGU_BLOCK ** -0.5),
        "od_b_s": gain((N_ODD, SGU_GROUPS, SGU_BLOCK)),
        "od_w_out": nrm((N_ODD, SGU_HALF, D_MODEL), SGU_HALF ** -0.5 * BETA),
        "ln_mix_g": gain((DEPTH, D_MODEL)),
        "ln_mix_b": nrm((DEPTH, D_MODEL), 0.02),
        "ln_ffn_g": gain((DEPTH, D_MODEL)),
        "ln_ffn_b": nrm((DEPTH, D_MODEL), 0.02),
        "router_w": nrm((D_MODEL, N_EXPERTS), D_MODEL ** -0.5),
        "router_b": nrm((N_EXPERTS,), 0.01),
        "moe_w_gate": nrm((DEPTH, N_EXPERTS, D_MODEL, D_EXPERT), D_MODEL ** -0.5),
        "moe_w_up": nrm((DEPTH, N_EXPERTS, D_MODEL, D_EXPERT), D_MODEL ** -0.5),
        "moe_w_down": nrm((DEPTH, N_EXPERTS, D_EXPERT, D_MODEL), D_EXPERT ** -0.5 * BETA),
    }


def reference(x, positions, ev_w_in, ev_q_norm, ev_kv_norm, ev_w_uq, ev_w_ukv, ev_w_gk2,
              ev_b_gk2, ev_gla_norm, ev_w_out, od_w_in, od_sgu_ln_g, od_sgu_ln_b, od_w_s,
              od_b_s, od_w_out, ln_mix_g, ln_mix_b, ln_ffn_g, ln_ffn_b, router_w, router_b,
              moe_w_gate, moe_w_up, moe_w_down):
    cos, sin = rope_tables(positions)
    for i in range(DEPTH):
        j = i // 2
        if i % 2 == 0:
            h = x @ ev_w_in[j]
            c_q, c_kv, k_r, g_q, g_k, g_v, g_lr, g_o = jnp.split(h, EVEN_SPLITS, axis=-1)
            o_a = mla(c_q, c_kv, k_r, ev_q_norm[j], ev_kv_norm[j], ev_w_uq[j], ev_w_ukv[j], cos, sin)
            o_b = gla(g_q, g_k, g_v, g_lr, g_o, ev_w_gk2[j], ev_b_gk2[j], ev_gla_norm[j])
            mix = jnp.concatenate([o_a, o_b], axis=-1) @ ev_w_out[j]
        else:
            h = jax.nn.gelu(x @ od_w_in[j])
            mix = spatial_gating(h, od_sgu_ln_g[j], od_sgu_ln_b[j], od_w_s[j], od_b_s[j]) @ od_w_out[j]
        x = layer_norm(ALPHA * x + mix, ln_mix_g[i], ln_mix_b[i])
        ffn = grouped_moe(x, router_w, router_b, moe_w_gate[i], moe_w_up[i], moe_w_down[i])
        x = layer_norm(ALPHA * x + ffn, ln_ffn_g[i], ln_ffn_b[i])
    return x
```

```python
import functools

import jax
import jax.numpy as jnp
import numpy as np
from jax import lax
from jax.experimental import pallas as pl
from jax.experimental.pallas import tpu as pltpu

F32 = jnp.float32
BF16 = jnp.bfloat16

D_MODEL = 2048
DEPTH = 2
CHUNK = 64
MLA_HEADS = 8
MLA_RANK = 512
MLA_NOPE = 128
MLA_ROPE = 64
MLA_V = 128
MLA_QK_PAD = 256
ROPE_THETA = 10000.0
GLA_HEADS = 4
GLA_DK = 128
GLA_DV = 256
GLA_GATE_RANK = 16
GLA_GATE_NORM = 16.0
SGU_BLOCK = 128
SGU_HALF = D_MODEL
SGU_GROUPS = 16
N_EXPERTS = 16
N_GROUPS = 4
EXPERTS_PER_GROUP = 4
D_EXPERT = 512
ALPHA = (2.0 * DEPTH) ** 0.25
EPS = 1e-6
NEG_BIG = -1e30

LANE = 128
VMEM_LIMIT = 56 * 1024 * 1024

COL_CQ, COL_CKV, COL_GQ, COL_GK = 0, 512, 1024, 1536
COL_GV, COL_GO, COL_KR, COL_GLR = 2048, 3072, 4096, 4224
EVEN_IN_PAD = 4352


def _dot(a, b):
    return jnp.dot(a, b, preferred_element_type=F32)


def _dot_nt(a, b):
    return lax.dot_general(a, b, (((1,), (1,)), ((), ())), preferred_element_type=F32)


def _dot_tn(a, b):
    return lax.dot_general(a, b, (((0,), (0,)), ((), ())), preferred_element_type=F32)


def _split_hi_lo(x):
    hi = x.astype(BF16)
    lo = (x - hi.astype(F32)).astype(BF16)
    return hi, lo


def _layer_norm(y, g, b):
    mu = jnp.mean(y, axis=-1, keepdims=True)
    d = y - mu
    var = jnp.mean(d * d, axis=-1, keepdims=True)
    return d * lax.rsqrt(var + EPS) * g + b


def _rms_norm(y, g):
    return y * lax.rsqrt(jnp.mean(y * y, axis=-1, keepdims=True) + EPS) * g


def _sigmoid(x):
    return 1.0 / (1.0 + jnp.exp(-x))


def _params(sem):
    return pltpu.CompilerParams(dimension_semantics=sem, vmem_limit_bytes=VMEM_LIMIT)


def _resident(shape):
    nd = len(shape)
    return pl.BlockSpec(shape, lambda *_: (0,) * nd, pipeline_mode=pl.Buffered(1))


def _inproj_kernel(x_ref, w_ref, o_ref, *, gelu, n_chunk):
    xb = x_ref[...].astype(BF16)
    n = w_ref.shape[1]
    for n0 in range(0, n, n_chunk):
        nw = min(n_chunk, n - n0)
        y = _dot(xb, w_ref[:, n0:n0 + nw])
        if gelu:
            y = 0.5 * y * (1.0 + jnp.tanh(0.7978845608028654 * (y + 0.044715 * (y * y * y))))
        o_ref[:, n0:n0 + nw] = y.astype(o_ref.dtype)


def _inproj(x, w, *, gelu, tm=512):
    t, d = x.shape
    n = w.shape[1]
    return pl.pallas_call(
        functools.partial(_inproj_kernel, gelu=gelu, n_chunk=512),
        out_shape=jax.ShapeDtypeStruct((t, n), BF16),
        grid=(t // tm,),
        in_specs=[pl.BlockSpec((tm, d), lambda i: (i, 0)), _resident((d, n))],
        out_specs=pl.BlockSpec((tm, n), lambda i: (i, 0)),
        compiler_params=_params(("parallel",)),
        name="inproj_gelu" if gelu else "inproj",
    )(x, w)


def _mla_proj_kernel(cq_ref, ckv_ref, kr_ref, pos_ref, inv_ref, gq_ref, gkv_ref,
                     wqa_ref, wqb_ref, wkv_ref, q_ref, kn_ref, krot_ref, v_ref):
    scale = (MLA_NOPE + MLA_ROPE) ** -0.5
    ang = pos_ref[...].astype(F32) * inv_ref[...]
    lane = lax.broadcasted_iota(jnp.int32, ang.shape, 1)
    rot_lane = lane < MLA_ROPE
    cos = jnp.where(rot_lane, jnp.cos(ang), 0.0)
    sin = jnp.where(rot_lane, jnp.sin(ang), 0.0)

    nq = _rms_norm(cq_ref[...].astype(F32), gq_ref[...]).astype(BF16)
    for h in range(MLA_HEADS):
        a = _dot(nq, wqa_ref[:, h * MLA_QK_PAD:(h + 1) * MLA_QK_PAD])
        b = _dot(nq, wqb_ref[:, h * LANE:(h + 1) * LANE])
        q_ref[:, h * MLA_QK_PAD:h * MLA_QK_PAD + LANE] = (a[:, :LANE] * scale).astype(BF16)
        q_ref[:, h * MLA_QK_PAD + LANE:(h + 1) * MLA_QK_PAD] = (
            (a[:, LANE:] * cos + b * sin) * scale).astype(BF16)

    nkv = _rms_norm(ckv_ref[...].astype(F32), gkv_ref[...]).astype(BF16)
    width = MLA_HEADS * MLA_NOPE
    kn_ref[...] = _dot(nkv, wkv_ref[:, :width]).astype(BF16)
    v_ref[...] = _dot(nkv, wkv_ref[:, width:]).astype(BF16)

    kr = kr_ref[...].astype(F32)
    krot_ref[...] = (kr * cos + pltpu.roll(kr, MLA_ROPE, axis=1) * sin).astype(BF16)


def _mla_proj(h, pos, inv, gq, gkv, wqa, wqb, wkv, *, tm=512):
    t = h.shape[0]
    hw = MLA_HEADS * MLA_NOPE
    return pl.pallas_call(
        _mla_proj_kernel,
        out_shape=(jax.ShapeDtypeStruct((t, MLA_HEADS * MLA_QK_PAD), BF16),
                   jax.ShapeDtypeStruct((t, hw), BF16),
                   jax.ShapeDtypeStruct((t, LANE), BF16),
                   jax.ShapeDtypeStruct((t, hw), BF16)),
        grid=(t // tm,),
        in_specs=[pl.BlockSpec((tm, MLA_RANK), lambda i: (i, COL_CQ // MLA_RANK)),
                  pl.BlockSpec((tm, MLA_RANK), lambda i: (i, COL_CKV // MLA_RANK)),
                  pl.BlockSpec((tm, LANE), lambda i: (i, COL_KR // LANE)),
                  pl.BlockSpec((tm, 1), lambda i: (i, 0)),
                  _resident((1, LANE)), _resident((1, MLA_RANK)), _resident((1, MLA_RANK)),
                  _resident(wqa.shape), _resident(wqb.shape), _resident(wkv.shape)],
        out_specs=(pl.BlockSpec((tm, MLA_HEADS * MLA_QK_PAD), lambda i: (i, 0)),
                   pl.BlockSpec((tm, hw), lambda i: (i, 0)),
                   pl.BlockSpec((tm, LANE), lambda i: (i, 0)),
                   pl.BlockSpec((tm, hw), lambda i: (i, 0))),
        compiler_params=_params(("parallel",)),
        name="mla_proj",
    )(h, h, h, pos, inv, gq, gkv, wqa, wqb, wkv)


def _attn_kernel(qi_ref, ki_ref, last_ref, q_ref, kn_ref, krot_ref, v_ref, o_ref,
                 m_ref, l_ref, acc_ref, *, tq, tk):
    p_id = pl.program_id(1)
    qi = qi_ref[p_id]
    ki = ki_ref[p_id]
    is_last = last_ref[p_id] == 1

    @pl.when(ki == 0)
    def _():
        m_ref[...] = jnp.full(m_ref.shape, NEG_BIG, F32)
        l_ref[...] = jnp.zeros(l_ref.shape, F32)
        acc_ref[...] = jnp.zeros(acc_ref.shape, F32)

    def step(masked):
        krot = krot_ref[...]
        if masked:
            q_chunk = (qi * tq + lax.broadcasted_iota(jnp.int32, (tq, tk), 0)) // CHUNK
            k_chunk = (ki * tk + lax.broadcasted_iota(jnp.int32, (tq, tk), 1)) // CHUNK
            visible = k_chunk <= q_chunk
        for h in range(MLA_HEADS):
            qh = q_ref[:, h * MLA_QK_PAD:(h + 1) * MLA_QK_PAD]
            kh = jnp.concatenate([kn_ref[:, h * MLA_NOPE:(h + 1) * MLA_NOPE], krot], axis=1)
            s = _dot_nt(qh, kh)
            if masked:
                s = jnp.where(visible, s, NEG_BIG)
            m_prev = m_ref[h]
            m_new = jnp.maximum(m_prev, jnp.max(s, axis=-1, keepdims=True))
            alpha = jnp.exp(m_prev - m_new)
            p = jnp.exp(s - m_new)
            l_ref[h] = alpha * l_ref[h] + jnp.sum(p, axis=-1, keepdims=True)
            m_ref[h] = m_new
            pv = _dot(p.astype(BF16), v_ref[:, h * MLA_V:(h + 1) * MLA_V])
            acc_ref[:, h * MLA_V:(h + 1) * MLA_V] = alpha * acc_ref[:, h * MLA_V:(h + 1) * MLA_V] + pv

    @pl.when(jnp.logical_not(is_last))
    def _():
        step(False)

    @pl.when(is_last)
    def _():
        step(True)
        for h in range(MLA_HEADS):
            o_ref[:, h * MLA_V:(h + 1) * MLA_V] = (
                acc_ref[:, h * MLA_V:(h + 1) * MLA_V] / l_ref[h]).astype(o_ref.dtype)


def _attn_pairs(seq, tq, tk):
    qi, ki, last = [], [], []
    for i in range(seq // tq):
        n_kv = ((i + 1) * tq - 1) // tk + 1
        for j in range(n_kv):
            qi.append(i)
            ki.append(j)
            last.append(1 if j == n_kv - 1 else 0)
    return (np.asarray(qi, np.int32), np.asarray(ki, np.int32), np.asarray(last, np.int32))


def _attention(q, kn, krot, v, batch, seq, *, tq=256, tk=1024):
    qi, ki, last = _attn_pairs(seq, tq, tk)
    nq, nk = seq // tq, seq // tk
    hw = MLA_HEADS * MLA_V
    grid_spec = pltpu.PrefetchScalarGridSpec(
        num_scalar_prefetch=3,
        grid=(batch, len(qi)),
        in_specs=[
            pl.BlockSpec((tq, MLA_HEADS * MLA_QK_PAD), lambda b, p, qi, ki, la: (b * nq + qi[p], 0)),
            pl.BlockSpec((tk, hw), lambda b, p, qi, ki, la: (b * nk + ki[p], 0)),
            pl.BlockSpec((tk, LANE), lambda b, p, qi, ki, la: (b * nk + ki[p], 0)),
            pl.BlockSpec((tk, hw), lambda b, p, qi, ki, la: (b * nk + ki[p], 0)),
        ],
        out_specs=pl.BlockSpec((tq, hw), lambda b, p, qi, ki, la: (b * nq + qi[p], 0)),
        scratch_shapes=[pltpu.VMEM((MLA_HEADS, tq, 1), F32),
                        pltpu.VMEM((MLA_HEADS, tq, 1), F32),
                        pltpu.VMEM((tq, hw), F32)],
    )
    return pl.pallas_call(
        functools.partial(_attn_kernel, tq=tq, tk=tk),
        out_shape=jax.ShapeDtypeStruct((batch * seq, hw), BF16),
        grid_spec=grid_spec,
        compiler_params=_params(("parallel", "arbitrary")),
        name="mla_attention",
    )(jnp.asarray(qi), jnp.asarray(ki), jnp.asarray(last), q, kn, krot, v)


def _gla_kernel(gq_ref, gk_ref, gv_ref, go_ref, glr_ref, whi_ref, wlo_ref, bg_ref, gn_ref,
                o_ref, st_ref, *, n_chunks):
    @pl.when(pl.program_id(1) == 0)
    def _():
        st_ref[...] = jnp.zeros(st_ref.shape, F32)

    row = lax.broadcasted_iota(jnp.int32, (CHUNK, CHUNK), 0)
    col = lax.broadcasted_iota(jnp.int32, (CHUNK, CHUNK), 1)
    tril = jnp.where(col <= row, 1.0, 0.0).astype(BF16)
    q_scale = GLA_DK ** -0.5

    def chunk_body(c, carry):
        rows = pl.ds(pl.multiple_of(c * CHUNK, CHUNK), CHUNK)
        glr = glr_ref[rows, :]
        logit = _dot(glr, whi_ref[...]) + _dot(glr, wlo_ref[...]) + bg_ref[...]
        lg = -(jnp.maximum(-logit, 0.0) + jnp.log1p(jnp.exp(-jnp.abs(logit)))) / GLA_GATE_NORM
        lg_hi, lg_lo = _split_hi_lo(lg)
        cum = _dot(tril, lg_hi) + _dot(tril, lg_lo)
        cum_end = cum[CHUNK - 1:CHUNK, :]
        k_dec = (gk_ref[rows, :].astype(F32) * jnp.exp(cum_end - cum)).astype(BF16)
        decay = jnp.exp(cum_end)
        qs = (gq_ref[rows, :].astype(F32) * q_scale).astype(BF16)
        for h in range(GLA_HEADS):
            ks = slice(h * GLA_DK, (h + 1) * GLA_DK)
            vs = slice(h * GLA_DV, (h + 1) * GLA_DV)
            upd_t = _dot_tn(gv_ref[rows, vs], k_dec[:, ks])
            st = st_ref[h] * decay[:, ks] + upd_t
            st_ref[h] = st
            o = _dot_nt(qs[:, ks], st.astype(BF16))
            o = _rms_norm(o, gn_ref[...])
            g = go_ref[rows, vs].astype(F32)
            o_ref[rows, vs] = (o * (g * _sigmoid(g))).astype(o_ref.dtype)
        return carry

    lax.fori_loop(0, n_chunks, chunk_body, 0)


def _gla(h, whi, wlo, bg, gn, batch, seq, *, tc=512):
    t = h.shape[0]
    nb = seq // tc
    kw, vw = GLA_HEADS * GLA_DK, GLA_HEADS * GLA_DV
    return pl.pallas_call(
        functools.partial(_gla_kernel, n_chunks=tc // CHUNK),
        out_shape=jax.ShapeDtypeStruct((t, vw), BF16),
        grid=(batch, nb),
        in_specs=[pl.BlockSpec((tc, kw), lambda b, i: (b * nb + i, COL_GQ // kw)),
                  pl.BlockSpec((tc, kw), lambda b, i: (b * nb + i, COL_GK // kw)),
                  pl.BlockSpec((tc, vw), lambda b, i: (b * nb + i, COL_GV // vw)),
                  pl.BlockSpec((tc, vw), lambda b, i: (b * nb + i, COL_GO // vw)),
                  pl.BlockSpec((tc, LANE), lambda b, i: (b * nb + i, COL_GLR // LANE)),
                  _resident(whi.shape), _resident(wlo.shape), _resident(bg.shape),
                  _resident(gn.shape)],
        out_specs=pl.BlockSpec((tc, vw), lambda b, i: (b * nb + i, 0)),
        scratch_shapes=[pltpu.VMEM((GLA_HEADS, GLA_DV, GLA_DK), F32)],
        compiler_params=_params(("parallel", "arbitrary")),
        name="gla",
    )(h, h, h, h, h, whi, wlo, bg, gn)


def _sgu_kernel(u_ref, v_ref, lg_ref, lb_ref, ws_ref, bs_ref, o_ref):
    tm = u_ref.shape[0]
    vn = _layer_norm(v_ref[...].astype(F32), lg_ref[...], lb_ref[...]).astype(BF16)
    pos_chunk_i = lax.broadcasted_iota(jnp.int32, (SGU_BLOCK, SGU_BLOCK), 0) // CHUNK
    pos_chunk_j = lax.broadcasted_iota(jnp.int32, (SGU_BLOCK, SGU_BLOCK), 1) // CHUNK
    causal = pos_chunk_j <= pos_chunk_i
    for g in range(SGU_GROUPS):
        cs = slice(g * LANE, (g + 1) * LANE)
        w = jnp.where(causal, ws_ref[g], 0.0).astype(BF16)
        for n in range(tm // SGU_BLOCK):
            rs = slice(n * SGU_BLOCK, (n + 1) * SGU_BLOCK)
            mixed = _dot(w, vn[rs, cs]) + bs_ref[:, cs]
            o_ref[rs, cs] = (u_ref[rs, cs].astype(F32) * mixed).astype(o_ref.dtype)


def _sgu(h, lg, lb, ws, bs_full, *, tm=256):
    t = h.shape[0]
    return pl.pallas_call(
        _sgu_kernel,
        out_shape=jax.ShapeDtypeStruct((t, SGU_HALF), BF16),
        grid=(t // tm,),
        in_specs=[pl.BlockSpec((tm, SGU_HALF), lambda i: (i, 0)),
                  pl.BlockSpec((tm, SGU_HALF), lambda i: (i, 1)),
                  _resident(lg.shape), _resident(lb.shape), _resident(ws.shape),
                  _resident(bs_full.shape)],
        out_specs=pl.BlockSpec((tm, SGU_HALF), lambda i: (i, 0)),
        compiler_params=_params(("parallel",)),
        name="sgu",
    )(h, h, lg, lb, ws, bs_full)


def _route(x1, rw2_ref, rwhi_ref, rb_ref):
    x_hi, x_lo = _split_hi_lo(x1)
    part = _dot_nt(rw2_ref[...], x_hi)
    logits = part[:N_EXPERTS] + part[N_EXPERTS:] + _dot_nt(rwhi_ref[...], x_lo)
    scores = _sigmoid(logits)
    biased = scores + rb_ref[...]
    b = [biased[e:e + 1, :] for e in range(N_EXPERTS)]
    s = [scores[e:e + 1, :] for e in range(N_EXPERTS)]

    def top2_sum(v):
        best = v[0] + v[1]
        for i in range(len(v)):
            for j in range(i + 1, len(v)):
                if (i, j) != (0, 1):
                    best = jnp.maximum(best, v[i] + v[j])
        return best

    def argmax_first(v):
        best_v, best_i = v[0], jnp.zeros(v[0].shape, jnp.int32)
        for i in range(1, len(v)):
            upd = v[i] > best_v
            best_i = jnp.where(upd, i, best_i)
            best_v = jnp.where(upd, v[i], best_v)
        return best_i

    def pick(v, idx):
        out = v[len(v) - 1]
        for i in range(len(v) - 2, -1, -1):
            out = jnp.where(idx == i, v[i], out)
        return out

    grp = [top2_sum(b[g * EXPERTS_PER_GROUP:(g + 1) * EXPERTS_PER_GROUP]) for g in range(N_GROUPS)]
    best_g = argmax_first(grp)
    cand_b = [pick([b[g * EXPERTS_PER_GROUP + j] for g in range(N_GROUPS)], best_g)
              for j in range(EXPERTS_PER_GROUP)]
    cand_s = [pick([s[g * EXPERTS_PER_GROUP + j] for g in range(N_GROUPS)], best_g)
              for j in range(EXPERTS_PER_GROUP)]
    i1 = argmax_first(cand_b)
    cand_b2 = [jnp.where(i1 == j, -jnp.inf, cand_b[j]) for j in range(EXPERTS_PER_GROUP)]
    i2 = argmax_first(cand_b2)
    g1 = pick(cand_s, i1)
    g2 = pick(cand_s, i2)
    den = g1 + g2
    e1 = best_g * EXPERTS_PER_GROUP + i1
    e2 = best_g * EXPERTS_PER_GROUP + i2
    rows = [jnp.where(e1 == e, g1 / den, 0.0) + jnp.where(e2 == e, g2 / den, 0.0)
            for e in range(N_EXPERTS)]
    return jnp.concatenate(rows, axis=0)


def _outproj_kernel(*refs, n_in):
    a_refs = refs[:n_in]
    w_refs = refs[n_in:2 * n_in]
    x_ref, g_ref, b_ref, rw2_ref, rwhi_ref, rb_ref, x1_ref, x1b_ref, comb_ref = refs[2 * n_in:]
    mix = _dot(a_refs[0][...], w_refs[0][...])
    for a_ref, w_ref in zip(a_refs[1:], w_refs[1:]):
        mix = mix + _dot(a_ref[...], w_ref[...])
    x1 = _layer_norm(ALPHA * x_ref[...] + mix, g_ref[...], b_ref[...])
    x1_ref[...] = x1
    x1b_ref[...] = x1.astype(BF16)
    comb_ref[...] = _route(x1, rw2_ref, rwhi_ref, rb_ref)


def _outproj(acts, ws, x, g, b, rw2, rwhi, rb, *, tm=256):
    t, d = x.shape
    n_in = len(acts)
    in_specs = ([pl.BlockSpec((tm, a.shape[1]), lambda i: (i, 0)) for a in acts]
                + [_resident(w.shape) for w in ws]
                + [pl.BlockSpec((tm, d), lambda i: (i, 0)),
                   _resident(g.shape), _resident(b.shape), _resident(rw2.shape),
                   _resident(rwhi.shape), _resident(rb.shape)])
    return pl.pallas_call(
        functools.partial(_outproj_kernel, n_in=n_in),
        out_shape=(jax.ShapeDtypeStruct((t, d), F32),
                   jax.ShapeDtypeStruct((t, d), BF16),
                   jax.ShapeDtypeStruct((N_EXPERTS, t), F32)),
        grid=(t // tm,),
        in_specs=in_specs,
        out_specs=(pl.BlockSpec((tm, d), lambda i: (i, 0)),
                   pl.BlockSpec((tm, d), lambda i: (i, 0)),
                   pl.BlockSpec((N_EXPERTS, tm), lambda i: (0, i))),
        compiler_params=_params(("parallel",)),
        name="outproj_ln_route",
    )(*acts, *ws, x, g, b, rw2, rwhi, rb)


def _moe_kernel(xb_ref, x_ref, comb_ref, wg_ref, wu_ref, wd_ref, g_ref, b_ref, o_ref, acc_ref):
    e = pl.program_id(1)

    @pl.when(e == 0)
    def _():
        acc_ref[...] = jnp.zeros(acc_ref.shape, F32)

    xb = xb_ref[...]
    gate = _dot(xb, wg_ref[0])
    up = _dot(xb, wu_ref[0])
    hidden = (gate * _sigmoid(gate)) * up * comb_ref[0]
    acc_ref[...] += _dot(hidden.astype(BF16), wd_ref[0])

    @pl.when(e == N_EXPERTS - 1)
    def _():
        o_ref[...] = _layer_norm(ALPHA * x_ref[...] + acc_ref[...], g_ref[...], b_ref[...])


def _moe(xb, x, comb, wg, wu, wd, g, b, *, tm=512):
    t, d = x.shape
    return pl.pallas_call(
        _moe_kernel,
        out_shape=jax.ShapeDtypeStruct((t, d), F32),
        grid=(t // tm, N_EXPERTS),
        in_specs=[pl.BlockSpec((tm, d), lambda i, e: (i, 0)),
                  pl.BlockSpec((tm, d), lambda i, e: (i, 0)),
                  pl.BlockSpec((1, tm, 1), lambda i, e: (e, i, 0)),
                  pl.BlockSpec((1, d, D_EXPERT), lambda i, e: (e, 0, 0)),
                  pl.BlockSpec((1, d, D_EXPERT), lambda i, e: (e, 0, 0)),
                  pl.BlockSpec((1, D_EXPERT, d), lambda i, e: (e, 0, 0)),
                  _resident(g.shape), _resident(b.shape)],
        out_specs=pl.BlockSpec((tm, d), lambda i, e: (i, 0)),
        scratch_shapes=[pltpu.VMEM((tm, d), F32)],
        compiler_params=_params(("parallel", "arbitrary")),
        name="moe_ln",
    )(xb, x, comb, wg, wu, wd, g, b)


def _rotate_half_cols(w):
    half = MLA_ROPE // 2
    return jnp.concatenate([-w[..., half:], w[..., :half]], axis=-1)


def _pack_even_in(w_in):
    cq, ckv, kr, gq, gk, gv, glr, go = jnp.split(
        w_in, np.cumsum([512, 512, 64, 512, 512, 1024, 16])[:].tolist(), axis=1)
    pad = jnp.zeros((w_in.shape[0], EVEN_IN_PAD - COL_GLR - GLA_GATE_RANK), w_in.dtype)
    return jnp.concatenate([cq, ckv, gq, gk, gv, go, kr, _rotate_half_cols(kr), glr, pad],
                           axis=1).astype(BF16)


def _pack_wq(w_uq):
    r = w_uq.shape[0]
    nope, rope = w_uq[..., :MLA_NOPE], w_uq[..., MLA_NOPE:]
    zeros = jnp.zeros((r, MLA_HEADS, MLA_QK_PAD - MLA_NOPE - MLA_ROPE), w_uq.dtype)
    wqa = jnp.concatenate([nope, rope, zeros], axis=-1).reshape(r, MLA_HEADS * MLA_QK_PAD)
    wqb = jnp.concatenate([_rotate_half_cols(rope), zeros], axis=-1).reshape(r, MLA_HEADS * LANE)
    return wqa.astype(BF16), wqb.astype(BF16)


def _pack_wkv(w_ukv):
    r = w_ukv.shape[0]
    kn = w_ukv[..., :MLA_NOPE].reshape(r, MLA_HEADS * MLA_NOPE)
    v = w_ukv[..., MLA_NOPE:].reshape(r, MLA_HEADS * MLA_V)
    return jnp.concatenate([kn, v], axis=1).astype(BF16)


def kernel(x, positions, ev_w_in, ev_q_norm, ev_kv_norm, ev_w_uq, ev_w_ukv, ev_w_gk2, ev_b_gk2, ev_gla_norm, ev_w_out, od_w_in, od_sgu_ln_g, od_sgu_ln_b, od_w_s, od_b_s, od_w_out, ln_mix_g, ln_mix_b, ln_ffn_g, ln_ffn_b, router_w, router_b, moe_w_gate, moe_w_up, moe_w_down):
    batch, seq, d = x.shape
    t = batch * seq
    xf = x.reshape(t, d)
    pos = positions.reshape(t, 1)

    inv = 1.0 / (ROPE_THETA ** (jnp.arange(0, MLA_ROPE, 2, dtype=F32) / MLA_ROPE))
    inv_row = jnp.concatenate([inv, inv, jnp.zeros((LANE - MLA_ROPE,), F32)]).reshape(1, LANE)

    rw_t = router_w.T
    rw_hi = rw_t.astype(BF16)
    rw_lo = (rw_t - rw_hi.astype(F32)).astype(BF16)
    rw2 = jnp.concatenate([rw_hi, rw_lo], axis=0)
    rb = router_b.astype(F32).reshape(N_EXPERTS, 1)

    for i in range(DEPTH):
        j = i // 2
        if i % 2 == 0:
            h = _inproj(xf, _pack_even_in(ev_w_in[j]), gelu=False)
            wqa, wqb = _pack_wq(ev_w_uq[j])
            q, kn, krot, v = _mla_proj(h, pos, inv_row, ev_q_norm[j].reshape(1, -1),
                                       ev_kv_norm[j].reshape(1, -1), wqa, wqb,
                                       _pack_wkv(ev_w_ukv[j]))
            o_a = _attention(q, kn, krot, v, batch, seq)
            wg2 = jnp.concatenate(
                [ev_w_gk2[j], jnp.zeros((LANE - GLA_GATE_RANK, GLA_HEADS * GLA_DK), F32)], axis=0)
            wg2_hi = wg2.astype(BF16)
            wg2_lo = (wg2 - wg2_hi.astype(F32)).astype(BF16)
            o_b = _gla(h, wg2_hi, wg2_lo, ev_b_gk2[j].reshape(1, -1),
                       ev_gla_norm[j].reshape(1, -1), batch, seq)
            w_out = ev_w_out[j].astype(BF16)
            split = MLA_HEADS * MLA_V
            acts, ws = [o_a, o_b], [w_out[:split], w_out[split:]]
        else:
            h = _inproj(xf, od_w_in[j].astype(BF16), gelu=True)
            bs_full = jnp.repeat(od_b_s[j].T, LANE, axis=1)
            gated = _sgu(h, od_sgu_ln_g[j].reshape(1, -1), od_sgu_ln_b[j].reshape(1, -1),
                         od_w_s[j], bs_full)
            acts, ws = [gated], [od_w_out[j].astype(BF16)]
        x1, x1b, comb_t = _outproj(acts, ws, xf, ln_mix_g[i].reshape(1, -1),
                                   ln_mix_b[i].reshape(1, -1), rw2, rw_hi, rb)
        xf = _moe(x1b, x1, comb_t.reshape(N_EXPERTS, t, 1),
                  moe_w_gate[i].astype(BF16), moe_w_up[i].astype(BF16),
                  moe_w_down[i].astype(BF16),
                  ln_ffn_g[i].reshape(1, -1), ln_ffn_b[i].reshape(1, -1))
    return xf.reshape(batch, seq, d)
```

```python
import functools

import jax
import jax.numpy as jnp
import numpy as np
from jax import lax
from jax.experimental import pallas as pl
from jax.experimental.pallas import tpu as pltpu

F32 = jnp.float32
BF16 = jnp.bfloat16

D_MODEL = 2048
DEPTH = 2
CHUNK = 64
MLA_HEADS = 8
MLA_RANK = 512
MLA_NOPE = 128
MLA_ROPE = 64
MLA_V = 128
MLA_QK_PAD = 256
ROPE_THETA = 10000.0
GLA_HEADS = 4
GLA_DK = 128
GLA_DV = 256
GLA_GATE_RANK = 16
GLA_GATE_NORM = 16.0
SGU_BLOCK = 128
SGU_HALF = D_MODEL
SGU_GROUPS = 16
N_EXPERTS = 16
N_GROUPS = 4
EXPERTS_PER_GROUP = 4
D_EXPERT = 512
PAIR_SLOT_A = np.array([0, 0, 0, 1, 1, 3], np.int32)
PAIR_SLOT_B = np.array([1, 2, 3, 3, 2, 2], np.int32)
PAIRS_PER_GROUP = 6
N_CLASSES = N_GROUPS * PAIRS_PER_GROUP
CLASS_ROWS = 32
ROUTE_ROWS = 8
ALPHA = (2.0 * DEPTH) ** 0.25
EPS = 1e-6
NEG_BIG = -1e30

LANE = 128
VMEM_LIMIT = 56 * 1024 * 1024

COL_CQ, COL_CKV, COL_GQ, COL_GK = 0, 512, 1024, 1536
COL_GV, COL_GO, COL_KR, COL_GLR = 2048, 3072, 4096, 4224
EVEN_IN_PAD = 4352


def _dot(a, b):
    return jnp.dot(a, b, preferred_element_type=F32)


def _dot_nt(a, b):
    return lax.dot_general(a, b, (((1,), (1,)), ((), ())), preferred_element_type=F32)


def _dot_tn(a, b):
    return lax.dot_general(a, b, (((0,), (0,)), ((), ())), preferred_element_type=F32)


def _split_hi_lo(x):
    hi = x.astype(BF16)
    lo = (x - hi.astype(F32)).astype(BF16)
    return hi, lo


def _layer_norm(y, g, b):
    mu = jnp.mean(y, axis=-1, keepdims=True)
    d = y - mu
    var = jnp.mean(d * d, axis=-1, keepdims=True)
    return d * lax.rsqrt(var + EPS) * g + b


def _rms_norm(y, g):
    return y * lax.rsqrt(jnp.mean(y * y, axis=-1, keepdims=True) + EPS) * g


def _sigmoid(x):
    return 1.0 / (1.0 + jnp.exp(-x))


def _params(sem):
    return pltpu.CompilerParams(dimension_semantics=sem, vmem_limit_bytes=VMEM_LIMIT)


def _resident(shape):
    nd = len(shape)
    return pl.BlockSpec(shape, lambda *_: (0,) * nd, pipeline_mode=pl.Buffered(1))


def _inproj_kernel(x_ref, w_ref, o_ref, *, gelu, n_chunk):
    xb = x_ref[...].astype(BF16)
    n = w_ref.shape[1]
    for n0 in range(0, n, n_chunk):
        nw = min(n_chunk, n - n0)
        y = _dot(xb, w_ref[:, n0:n0 + nw])
        if gelu:
            y = 0.5 * y * (1.0 + jnp.tanh(0.7978845608028654 * (y + 0.044715 * (y * y * y))))
        o_ref[:, n0:n0 + nw] = y.astype(o_ref.dtype)


def _inproj(x, w, *, gelu, tm=512):
    t, d = x.shape
    n = w.shape[1]
    return pl.pallas_call(
        functools.partial(_inproj_kernel, gelu=gelu, n_chunk=512),
        out_shape=jax.ShapeDtypeStruct((t, n), BF16),
        grid=(t // tm,),
        in_specs=[pl.BlockSpec((tm, d), lambda i: (i, 0)), _resident((d, n))],
        out_specs=pl.BlockSpec((tm, n), lambda i: (i, 0)),
        compiler_params=_params(("parallel",)),
        name="inproj_gelu" if gelu else "inproj",
    )(x, w)


def _mla_proj_kernel(cq_ref, ckv_ref, kr_ref, pos_ref, inv_ref, gq_ref, gkv_ref,
                     wqa_ref, wqb_ref, wkv_ref, q_ref, kn_ref, krot_ref, v_ref):
    scale = (MLA_NOPE + MLA_ROPE) ** -0.5
    ang = pos_ref[...].astype(F32) * inv_ref[...]
    lane = lax.broadcasted_iota(jnp.int32, ang.shape, 1)
    rot_lane = lane < MLA_ROPE
    cos = jnp.where(rot_lane, jnp.cos(ang), 0.0)
    sin = jnp.where(rot_lane, jnp.sin(ang), 0.0)

    nq = _rms_norm(cq_ref[...].astype(F32), gq_ref[...]).astype(BF16)
    for h in range(MLA_HEADS):
        a = _dot(nq, wqa_ref[:, h * MLA_QK_PAD:(h + 1) * MLA_QK_PAD])
        b = _dot(nq, wqb_ref[:, h * LANE:(h + 1) * LANE])
        q_ref[:, h * MLA_QK_PAD:h * MLA_QK_PAD + LANE] = (a[:, :LANE] * scale).astype(BF16)
        q_ref[:, h * MLA_QK_PAD + LANE:(h + 1) * MLA_QK_PAD] = (
            (a[:, LANE:] * cos + b * sin) * scale).astype(BF16)

    nkv = _rms_norm(ckv_ref[...].astype(F32), gkv_ref[...]).astype(BF16)
    width = MLA_HEADS * MLA_NOPE
    kn_ref[...] = _dot(nkv, wkv_ref[:, :width]).astype(BF16)
    v_ref[...] = _dot(nkv, wkv_ref[:, width:]).astype(BF16)

    kr = kr_ref[...].astype(F32)
    krot_ref[...] = (kr * cos + pltpu.roll(kr, MLA_ROPE, axis=1) * sin).astype(BF16)


def _mla_proj(h, pos, inv, gq, gkv, wqa, wqb, wkv, *, tm=512):
    t = h.shape[0]
    hw = MLA_HEADS * MLA_NOPE
    return pl.pallas_call(
        _mla_proj_kernel,
        out_shape=(jax.ShapeDtypeStruct((t, MLA_HEADS * MLA_QK_PAD), BF16),
                   jax.ShapeDtypeStruct((t, hw), BF16),
                   jax.ShapeDtypeStruct((t, LANE), BF16),
                   jax.ShapeDtypeStruct((t, hw), BF16)),
        grid=(t // tm,),
        in_specs=[pl.BlockSpec((tm, MLA_RANK), lambda i: (i, COL_CQ // MLA_RANK)),
                  pl.BlockSpec((tm, MLA_RANK), lambda i: (i, COL_CKV // MLA_RANK)),
                  pl.BlockSpec((tm, LANE), lambda i: (i, COL_KR // LANE)),
                  pl.BlockSpec((tm, 1), lambda i: (i, 0)),
                  _resident((1, LANE)), _resident((1, MLA_RANK)), _resident((1, MLA_RANK)),
                  _resident(wqa.shape), _resident(wqb.shape), _resident(wkv.shape)],
        out_specs=(pl.BlockSpec((tm, MLA_HEADS * MLA_QK_PAD), lambda i: (i, 0)),
                   pl.BlockSpec((tm, hw), lambda i: (i, 0)),
                   pl.BlockSpec((tm, LANE), lambda i: (i, 0)),
                   pl.BlockSpec((tm, hw), lambda i: (i, 0))),
        compiler_params=_params(("parallel",)),
        name="mla_proj",
    )(h, h, h, pos, inv, gq, gkv, wqa, wqb, wkv)


def _attn_kernel(qi_ref, ki_ref, last_ref, q_ref, kn_ref, krot_ref, v_ref, o_ref,
                 m_ref, l_ref, acc_ref, *, tq, tk):
    p_id = pl.program_id(1)
    qi = qi_ref[p_id]
    ki = ki_ref[p_id]
    is_last = last_ref[p_id] == 1

    @pl.when(ki == 0)
    def _():
        m_ref[...] = jnp.full(m_ref.shape, NEG_BIG, F32)
        l_ref[...] = jnp.zeros(l_ref.shape, F32)
        acc_ref[...] = jnp.zeros(acc_ref.shape, F32)

    def step(masked):
        krot = krot_ref[...]
        if masked:
            q_chunk = (qi * tq + lax.broadcasted_iota(jnp.int32, (tq, tk), 0)) // CHUNK
            k_chunk = (ki * tk + lax.broadcasted_iota(jnp.int32, (tq, tk), 1)) // CHUNK
            visible = k_chunk <= q_chunk
        for h in range(MLA_HEADS):
            qh = q_ref[:, h * MLA_QK_PAD:(h + 1) * MLA_QK_PAD]
            kh = jnp.concatenate([kn_ref[:, h * MLA_NOPE:(h + 1) * MLA_NOPE], krot], axis=1)
            s = _dot_nt(qh, kh)
            if masked:
                s = jnp.where(visible, s, NEG_BIG)
            m_prev = m_ref[h]
            m_new = jnp.maximum(m_prev, jnp.max(s, axis=-1, keepdims=True))
            alpha = jnp.exp(m_prev - m_new)
            p = jnp.exp(s - m_new)
            l_ref[h] = alpha * l_ref[h] + jnp.sum(p, axis=-1, keepdims=True)
            m_ref[h] = m_new
            pv = _dot(p.astype(BF16), v_ref[:, h * MLA_V:(h + 1) * MLA_V])
            acc_ref[:, h * MLA_V:(h + 1) * MLA_V] = alpha * acc_ref[:, h * MLA_V:(h + 1) * MLA_V] + pv

    @pl.when(jnp.logical_not(is_last))
    def _():
        step(False)

    @pl.when(is_last)
    def _():
        step(True)
        for h in range(MLA_HEADS):
            o_ref[:, h * MLA_V:(h + 1) * MLA_V] = (
                acc_ref[:, h * MLA_V:(h + 1) * MLA_V] / l_ref[h]).astype(o_ref.dtype)


def _attn_pairs(seq, tq, tk):
    qi, ki, last = [], [], []
    for i in range(seq // tq):
        n_kv = ((i + 1) * tq - 1) // tk + 1
        for j in range(n_kv):
            qi.append(i)
            ki.append(j)
            last.append(1 if j == n_kv - 1 else 0)
    return (np.asarray(qi, np.int32), np.asarray(ki, np.int32), np.asarray(last, np.int32))


def _attention(q, kn, krot, v, batch, seq, *, tq=256, tk=1024):
    qi, ki, last = _attn_pairs(seq, tq, tk)
    nq, nk = seq // tq, seq // tk
    hw = MLA_HEADS * MLA_V
    grid_spec = pltpu.PrefetchScalarGridSpec(
        num_scalar_prefetch=3,
        grid=(batch, len(qi)),
        in_specs=[
            pl.BlockSpec((tq, MLA_HEADS * MLA_QK_PAD), lambda b, p, qi, ki, la: (b * nq + qi[p], 0)),
            pl.BlockSpec((tk, hw), lambda b, p, qi, ki, la: (b * nk + ki[p], 0)),
            pl.BlockSpec((tk, LANE), lambda b, p, qi, ki, la: (b * nk + ki[p], 0)),
            pl.BlockSpec((tk, hw), lambda b, p, qi, ki, la: (b * nk + ki[p], 0)),
        ],
        out_specs=pl.BlockSpec((tq, hw), lambda b, p, qi, ki, la: (b * nq + qi[p], 0)),
        scratch_shapes=[pltpu.VMEM((MLA_HEADS, tq, 1), F32),
                        pltpu.VMEM((MLA_HEADS, tq, 1), F32),
                        pltpu.VMEM((tq, hw), F32)],
    )
    return pl.pallas_call(
        functools.partial(_attn_kernel, tq=tq, tk=tk),
        out_shape=jax.ShapeDtypeStruct((batch * seq, hw), BF16),
        grid_spec=grid_spec,
        compiler_params=_params(("parallel", "arbitrary")),
        name="mla_attention",
    )(jnp.asarray(qi), jnp.asarray(ki), jnp.asarray(last), q, kn, krot, v)


def _gla_kernel(gq_ref, gk_ref, gv_ref, go_ref, glr_ref, whi_ref, wlo_ref, bg_ref, gn_ref,
                o_ref, st_ref, *, n_chunks):
    @pl.when(pl.program_id(1) == 0)
    def _():
        st_ref[...] = jnp.zeros(st_ref.shape, F32)

    row = lax.broadcasted_iota(jnp.int32, (CHUNK, CHUNK), 0)
    col = lax.broadcasted_iota(jnp.int32, (CHUNK, CHUNK), 1)
    tril = jnp.where(col <= row, 1.0, 0.0).astype(BF16)
    q_scale = GLA_DK ** -0.5

    def chunk_body(c, carry):
        rows = pl.ds(pl.multiple_of(c * CHUNK, CHUNK), CHUNK)
        glr = glr_ref[rows, :]
        logit = _dot(glr, whi_ref[...]) + _dot(glr, wlo_ref[...]) + bg_ref[...]
        lg = -(jnp.maximum(-logit, 0.0) + jnp.log1p(jnp.exp(-jnp.abs(logit)))) / GLA_GATE_NORM
        lg_hi, lg_lo = _split_hi_lo(lg)
        cum = _dot(tril, lg_hi) + _dot(tril, lg_lo)
        cum_end = cum[CHUNK - 1:CHUNK, :]
        k_dec = (gk_ref[rows, :].astype(F32) * jnp.exp(cum_end - cum)).astype(BF16)
        decay = jnp.exp(cum_end)
        qs = (gq_ref[rows, :].astype(F32) * q_scale).astype(BF16)
        for h in range(GLA_HEADS):
            ks = slice(h * GLA_DK, (h + 1) * GLA_DK)
            vs = slice(h * GLA_DV, (h + 1) * GLA_DV)
            upd_t = _dot_tn(gv_ref[rows, vs], k_dec[:, ks])
            st = st_ref[h] * decay[:, ks] + upd_t
            st_ref[h] = st
            o = _dot_nt(qs[:, ks], st.astype(BF16))
            o = _rms_norm(o, gn_ref[...])
            g = go_ref[rows, vs].astype(F32)
            o_ref[rows, vs] = (o * (g * _sigmoid(g))).astype(o_ref.dtype)
        return carry

    lax.fori_loop(0, n_chunks, chunk_body, 0)


def _gla(h, whi, wlo, bg, gn, batch, seq, *, tc=512):
    t = h.shape[0]
    nb = seq // tc
    kw, vw = GLA_HEADS * GLA_DK, GLA_HEADS * GLA_DV
    return pl.pallas_call(
        functools.partial(_gla_kernel, n_chunks=tc // CHUNK),
        out_shape=jax.ShapeDtypeStruct((t, vw), BF16),
        grid=(batch, nb),
        in_specs=[pl.BlockSpec((tc, kw), lambda b, i: (b * nb + i, COL_GQ // kw)),
                  pl.BlockSpec((tc, kw), lambda b, i: (b * nb + i, COL_GK // kw)),
                  pl.BlockSpec((tc, vw), lambda b, i: (b * nb + i, COL_GV // vw)),
                  pl.BlockSpec((tc, vw), lambda b, i: (b * nb + i, COL_GO // vw)),
                  pl.BlockSpec((tc, LANE), lambda b, i: (b * nb + i, COL_GLR // LANE)),
                  _resident(whi.shape), _resident(wlo.shape), _resident(bg.shape),
                  _resident(gn.shape)],
        out_specs=pl.BlockSpec((tc, vw), lambda b, i: (b * nb + i, 0)),
        scratch_shapes=[pltpu.VMEM((GLA_HEADS, GLA_DV, GLA_DK), F32)],
        compiler_params=_params(("parallel", "arbitrary")),
        name="gla",
    )(h, h, h, h, h, whi, wlo, bg, gn)


def _sgu_kernel(u_ref, v_ref, lg_ref, lb_ref, ws_ref, bs_ref, o_ref):
    tm = u_ref.shape[0]
    vn = _layer_norm(v_ref[...].astype(F32), lg_ref[...], lb_ref[...]).astype(BF16)
    pos_chunk_i = lax.broadcasted_iota(jnp.int32, (SGU_BLOCK, SGU_BLOCK), 0) // CHUNK
    pos_chunk_j = lax.broadcasted_iota(jnp.int32, (SGU_BLOCK, SGU_BLOCK), 1) // CHUNK
    causal = pos_chunk_j <= pos_chunk_i
    for g in range(SGU_GROUPS):
        cs = slice(g * LANE, (g + 1) * LANE)
        w = jnp.where(causal, ws_ref[g], 0.0).astype(BF16)
        for n in range(tm // SGU_BLOCK):
            rs = slice(n * SGU_BLOCK, (n + 1) * SGU_BLOCK)
            mixed = _dot(w, vn[rs, cs]) + bs_ref[:, cs]
            o_ref[rs, cs] = (u_ref[rs, cs].astype(F32) * mixed).astype(o_ref.dtype)


def _sgu(h, lg, lb, ws, bs_full, *, tm=256):
    t = h.shape[0]
    return pl.pallas_call(
        _sgu_kernel,
        out_shape=jax.ShapeDtypeStruct((t, SGU_HALF), BF16),
        grid=(t // tm,),
        in_specs=[pl.BlockSpec((tm, SGU_HALF), lambda i: (i, 0)),
                  pl.BlockSpec((tm, SGU_HALF), lambda i: (i, 1)),
                  _resident(lg.shape), _resident(lb.shape), _resident(ws.shape),
                  _resident(bs_full.shape)],
        out_specs=pl.BlockSpec((tm, SGU_HALF), lambda i: (i, 0)),
        compiler_params=_params(("parallel",)),
        name="sgu",
    )(h, h, lg, lb, ws, bs_full)


def _route(x1, rw2_ref, rwhi_ref, rb_ref):
    x_hi, x_lo = _split_hi_lo(x1)
    part = _dot_nt(rw2_ref[...], x_hi)
    logits = part[:N_EXPERTS] + part[N_EXPERTS:] + _dot_nt(rwhi_ref[...], x_lo)
    scores = _sigmoid(logits)
    biased = scores + rb_ref[...]
    b = [biased[e:e + 1, :] for e in range(N_EXPERTS)]
    s = [scores[e:e + 1, :] for e in range(N_EXPERTS)]

    def top2_sum(v):
        best = v[0] + v[1]
        for i in range(len(v)):
            for j in range(i + 1, len(v)):
                if (i, j) != (0, 1):
                    best = jnp.maximum(best, v[i] + v[j])
        return best

    def argmax_first(v):
        best_v, best_i = v[0], jnp.zeros(v[0].shape, jnp.int32)
        for i in range(1, len(v)):
            upd = v[i] > best_v
            best_i = jnp.where(upd, i, best_i)
            best_v = jnp.where(upd, v[i], best_v)
        return best_i

    def pick(v, idx):
        out = v[len(v) - 1]
        for i in range(len(v) - 2, -1, -1):
            out = jnp.where(idx == i, v[i], out)
        return out

    grp = [top2_sum(b[g * EXPERTS_PER_GROUP:(g + 1) * EXPERTS_PER_GROUP]) for g in range(N_GROUPS)]
    best_g = argmax_first(grp)
    cand_b = [pick([b[g * EXPERTS_PER_GROUP + j] for g in range(N_GROUPS)], best_g)
              for j in range(EXPERTS_PER_GROUP)]
    cand_s = [pick([s[g * EXPERTS_PER_GROUP + j] for g in range(N_GROUPS)], best_g)
              for j in range(EXPERTS_PER_GROUP)]
    i1 = argmax_first(cand_b)
    cand_b2 = [jnp.where(i1 == j, -jnp.inf, cand_b[j]) for j in range(EXPERTS_PER_GROUP)]
    i2 = argmax_first(cand_b2)
    g1 = pick(cand_s, i1)
    g2 = pick(cand_s, i2)
    den = g1 + g2
    w1, w2 = g1 / den, g2 / den
    lo, hi = jnp.minimum(i1, i2), jnp.maximum(i1, i2)
    pair = jnp.where(lo == 0, hi - 1, jnp.where(lo == 1, jnp.where(hi == 3, 3, 4), 5))
    slot_a = jnp.where(pair < 3, 0, jnp.where(pair < 5, 1, 3))
    first_is_a = i1 == slot_a
    cls = best_g * PAIRS_PER_GROUP + pair
    return cls, jnp.where(first_is_a, w1, w2), jnp.where(first_is_a, w2, w1)


def _outproj_kernel(*refs, n_in):
    a_refs = refs[:n_in]
    w_refs = refs[n_in:2 * n_in]
    (x_ref, g_ref, b_ref, rw2_ref, rwhi_ref, rb_ref,
     x1_ref, ri_ref, rg_ref, cnt_ref, carry_ref) = refs[2 * n_in:]
    tm = x_ref.shape[0]

    @pl.when(pl.program_id(0) == 0)
    def _():
        carry_ref[...] = jnp.zeros(carry_ref.shape, F32)

    mix = _dot(a_refs[0][...], w_refs[0][...])
    for a_ref, w_ref in zip(a_refs[1:], w_refs[1:]):
        mix = mix + _dot(a_ref[...], w_ref[...])
    x1 = _layer_norm(ALPHA * x_ref[...] + mix, g_ref[...], b_ref[...])
    x1_ref[...] = x1

    cls, gate_a, gate_b = _route(x1, rw2_ref, rwhi_ref, rb_ref)
    class_row = lax.broadcasted_iota(jnp.int32, (CLASS_ROWS, tm), 0)
    onehot = jnp.where(class_row == cls, 1.0, 0.0)
    t_prev = lax.broadcasted_iota(jnp.int32, (tm, tm), 0)
    t_cur = lax.broadcasted_iota(jnp.int32, (tm, tm), 1)
    before = jnp.where(t_prev < t_cur, 1.0, 0.0).astype(BF16)
    seen = _dot(onehot.astype(BF16), before) + carry_ref[...]
    rank = jnp.sum(onehot * seen, axis=0, keepdims=True)
    carry_ref[...] += jnp.sum(onehot, axis=1, keepdims=True)

    pad_i = jnp.zeros((ROUTE_ROWS - 2, tm), jnp.int32)
    ri_ref[...] = jnp.concatenate([cls, rank.astype(jnp.int32), pad_i], axis=0)
    rg_ref[...] = jnp.concatenate([gate_a, gate_b, pad_i.astype(F32)], axis=0)
    cnt_ref[...] = jnp.broadcast_to(carry_ref[...], cnt_ref.shape).astype(jnp.int32)


def _outproj(acts, ws, x, g, b, rw2, rwhi, rb, *, tm=256):
    t, d = x.shape
    n_in = len(acts)
    in_specs = ([pl.BlockSpec((tm, a.shape[1]), lambda i: (i, 0)) for a in acts]
                + [_resident(w.shape) for w in ws]
                + [pl.BlockSpec((tm, d), lambda i: (i, 0)),
                   _resident(g.shape), _resident(b.shape), _resident(rw2.shape),
                   _resident(rwhi.shape), _resident(rb.shape)])
    return pl.pallas_call(
        functools.partial(_outproj_kernel, n_in=n_in),
        out_shape=(jax.ShapeDtypeStruct((t, d), F32),
                   jax.ShapeDtypeStruct((ROUTE_ROWS, t), jnp.int32),
                   jax.ShapeDtypeStruct((ROUTE_ROWS, t), F32),
                   jax.ShapeDtypeStruct((CLASS_ROWS, LANE), jnp.int32)),
        grid=(t // tm,),
        in_specs=in_specs,
        out_specs=(pl.BlockSpec((tm, d), lambda i: (i, 0)),
                   pl.BlockSpec((ROUTE_ROWS, tm), lambda i: (0, i)),
                   pl.BlockSpec((ROUTE_ROWS, tm), lambda i: (0, i)),
                   pl.BlockSpec((CLASS_ROWS, LANE), lambda i: (0, 0))),
        scratch_shapes=[pltpu.VMEM((CLASS_ROWS, 1), F32)],
        compiler_params=_params(("arbitrary",)),
        name="outproj_ln_route",
    )(*acts, *ws, x, g, b, rw2, rwhi, rb)


def _moe_plan(ri, rg, cnt, t, tm, n_tiles):
    cls, rank = ri[0], ri[1]
    counts = cnt[:N_CLASSES, 0]
    tiles_c = (counts + tm - 1) // tm
    tile_end = jnp.cumsum(tiles_c)
    tile_start = tile_end - tiles_c
    used = tile_end[-1]
    slot = tile_start[cls] * tm + rank
    inv = jnp.zeros((n_tiles * tm,), jnp.int32).at[slot].set(
        jnp.arange(t, dtype=jnp.int32), unique_indices=True)
    gates = jnp.zeros((n_tiles * tm, 2), F32).at[slot].set(
        jnp.stack([rg[0], rg[1]], axis=1), unique_indices=True)
    tile = jnp.arange(n_tiles, dtype=jnp.int32)
    covered = tile_end[None, :] <= jnp.minimum(tile, used - 1)[:, None]
    tile_cls = jnp.minimum(jnp.sum(covered.astype(jnp.int32), axis=1), N_CLASSES - 1)
    rows = jnp.clip(counts[tile_cls] - (tile - tile_start[tile_cls]) * tm, 0, tm)
    rows = jnp.where(tile < used, rows, 0).astype(jnp.int32)
    grp, pair = tile_cls // PAIRS_PER_GROUP, tile_cls % PAIRS_PER_GROUP
    expert_a = (grp * EXPERTS_PER_GROUP + jnp.asarray(PAIR_SLOT_A)[pair]).astype(jnp.int32)
    expert_b = (grp * EXPERTS_PER_GROUP + jnp.asarray(PAIR_SLOT_B)[pair]).astype(jnp.int32)
    return inv, gates, expert_a, expert_b, rows


def _moe_kernel(inv_ref, ea_ref, eb_ref, rows_ref, x_hbm, gs_ref,
                wga_ref, wua_ref, wda_ref, wgb_ref, wub_ref, wdb_ref, g_ref, b_ref,
                o_hbm, xbuf, obuf, gsem, ssem, *, tm, n_tiles):
    i = pl.program_id(0)
    cur = i % 2

    def row_copy_in(tile, r, buf):
        tok = inv_ref[tile * tm + r]
        return pltpu.make_async_copy(x_hbm.at[pl.ds(tok, 1), :],
                                     xbuf.at[buf, pl.ds(r, 1), :], gsem.at[buf])

    def row_copy_out(tile, r, buf):
        tok = inv_ref[tile * tm + r]
        return pltpu.make_async_copy(obuf.at[buf, pl.ds(r, 1), :],
                                     o_hbm.at[pl.ds(tok, 1), :], ssem.at[buf])

    def start_gather(tile, buf):
        def body(r, carry):
            row_copy_in(tile, r, buf).start()
            return carry
        lax.fori_loop(0, tm, body, 0, unroll=8)

    @pl.when(jnp.logical_and(i == 0, rows_ref[0] > 0))
    def _():
        start_gather(0, 0)

    nxt = jnp.minimum(i + 1, n_tiles - 1)

    @pl.when(jnp.logical_and(i + 1 < n_tiles, rows_ref[nxt] > 0))
    def _():
        start_gather(i + 1, 1 - cur)

    def wait_scatter(n_rows, buf):
        bit = 1
        while bit <= tm:
            @pl.when((n_rows & bit) != 0)
            def _(bit=bit):
                pltpu.make_async_copy(obuf.at[buf, pl.ds(0, bit), :],
                                      o_hbm.at[pl.ds(0, bit), :], ssem.at[buf]).wait()
            bit *= 2

    prev2 = jnp.maximum(i - 2, 0)

    @pl.when(jnp.logical_and(i >= 2, rows_ref[prev2] > 0))
    def _():
        wait_scatter(rows_ref[prev2], cur)

    n_rows = rows_ref[i]

    @pl.when(n_rows > 0)
    def _():
        pltpu.make_async_copy(x_hbm.at[pl.ds(0, tm), :], xbuf.at[cur], gsem.at[cur]).wait()
        x = xbuf[cur]
        xb = x.astype(BF16)
        gates = gs_ref[...]

        def expert(wg_ref, wu_ref, gate_col):
            gate = _dot(xb, wg_ref[0])
            up = _dot(xb, wu_ref[0])
            return ((gate * _sigmoid(gate)) * up * gate_col).astype(BF16)

        y = (_dot(expert(wga_ref, wua_ref, gates[:, 0:1]), wda_ref[0])
             + _dot(expert(wgb_ref, wub_ref, gates[:, 1:2]), wdb_ref[0]))
        obuf[cur] = _layer_norm(ALPHA * x + y, g_ref[...], b_ref[...])

        def body(r, carry):
            row_copy_out(i, r, cur).start()
            return carry
        lax.fori_loop(0, n_rows, body, 0)


def _moe(x, ri, rg, cnt, wg, wu, wd, g, b, *, tm=256):
    t, d = x.shape
    n_tiles = t // tm + N_CLASSES + 2
    inv, gates, expert_a, expert_b, rows = _moe_plan(ri, rg, cnt, t, tm, n_tiles)
    w_in = lambda sel: pl.BlockSpec((1, d, D_EXPERT), lambda i, inv, ea, eb, rows: (sel(ea, eb)[i], 0, 0))
    w_out = lambda sel: pl.BlockSpec((1, D_EXPERT, d), lambda i, inv, ea, eb, rows: (sel(ea, eb)[i], 0, 0))
    pick_a = lambda ea, eb: ea
    pick_b = lambda ea, eb: eb
    grid_spec = pltpu.PrefetchScalarGridSpec(
        num_scalar_prefetch=4,
        grid=(n_tiles,),
        in_specs=[pl.BlockSpec(memory_space=pl.ANY),
                  pl.BlockSpec((tm, 2), lambda i, *_: (i, 0)),
                  w_in(pick_a), w_in(pick_a), w_out(pick_a),
                  w_in(pick_b), w_in(pick_b), w_out(pick_b),
                  pl.BlockSpec(g.shape, lambda i, *_: (0, 0)),
                  pl.BlockSpec(b.shape, lambda i, *_: (0, 0))],
        out_specs=pl.BlockSpec(memory_space=pl.ANY),
        scratch_shapes=[pltpu.VMEM((2, tm, d), F32), pltpu.VMEM((2, tm, d), F32),
                        pltpu.SemaphoreType.DMA((2,)), pltpu.SemaphoreType.DMA((2,))],
    )
    return pl.pallas_call(
        functools.partial(_moe_kernel, tm=tm, n_tiles=n_tiles),
        out_shape=jax.ShapeDtypeStruct((t, d), F32),
        grid_spec=grid_spec,
        compiler_params=_params(("arbitrary",)),
        name="moe_ln",
    )(inv, expert_a, expert_b, rows, x, gates, wg, wu, wd, wg, wu, wd, g, b)


def _rotate_half_cols(w):
    half = MLA_ROPE // 2
    return jnp.concatenate([-w[..., half:], w[..., :half]], axis=-1)


def _pack_even_in(w_in):
    cq, ckv, kr, gq, gk, gv, glr, go = jnp.split(
        w_in, np.cumsum([512, 512, 64, 512, 512, 1024, 16])[:].tolist(), axis=1)
    pad = jnp.zeros((w_in.shape[0], EVEN_IN_PAD - COL_GLR - GLA_GATE_RANK), w_in.dtype)
    return jnp.concatenate([cq, ckv, gq, gk, gv, go, kr, _rotate_half_cols(kr), glr, pad],
                           axis=1).astype(BF16)


def _pack_wq(w_uq):
    r = w_uq.shape[0]
    nope, rope = w_uq[..., :MLA_NOPE], w_uq[..., MLA_NOPE:]
    zeros = jnp.zeros((r, MLA_HEADS, MLA_QK_PAD - MLA_NOPE - MLA_ROPE), w_uq.dtype)
    wqa = jnp.concatenate([nope, rope, zeros], axis=-1).reshape(r, MLA_HEADS * MLA_QK_PAD)
    wqb = jnp.concatenate([_rotate_half_cols(rope), zeros], axis=-1).reshape(r, MLA_HEADS * LANE)
    return wqa.astype(BF16), wqb.astype(BF16)


def _pack_wkv(w_ukv):
    r = w_ukv.shape[0]
    kn = w_ukv[..., :MLA_NOPE].reshape(r, MLA_HEADS * MLA_NOPE)
    v = w_ukv[..., MLA_NOPE:].reshape(r, MLA_HEADS * MLA_V)
    return jnp.concatenate([kn, v], axis=1).astype(BF16)


def kernel(x, positions, ev_w_in, ev_q_norm, ev_kv_norm, ev_w_uq, ev_w_ukv, ev_w_gk2, ev_b_gk2, ev_gla_norm, ev_w_out, od_w_in, od_sgu_ln_g, od_sgu_ln_b, od_w_s, od_b_s, od_w_out, ln_mix_g, ln_mix_b, ln_ffn_g, ln_ffn_b, router_w, router_b, moe_w_gate, moe_w_up, moe_w_down):
    batch, seq, d = x.shape
    t = batch * seq
    xf = x.reshape(t, d)
    pos = positions.reshape(t, 1)

    inv = 1.0 / (ROPE_THETA ** (jnp.arange(0, MLA_ROPE, 2, dtype=F32) / MLA_ROPE))
    inv_row = jnp.concatenate([inv, inv, jnp.zeros((LANE - MLA_ROPE,), F32)]).reshape(1, LANE)

    rw_t = router_w.T
    rw_hi = rw_t.astype(BF16)
    rw_lo = (rw_t - rw_hi.astype(F32)).astype(BF16)
    rw2 = jnp.concatenate([rw_hi, rw_lo], axis=0)
    rb = router_b.astype(F32).reshape(N_EXPERTS, 1)

    for i in range(DEPTH):
        j = i // 2
        if i % 2 == 0:
            h = _inproj(xf, _pack_even_in(ev_w_in[j]), gelu=False)
            wqa, wqb = _pack_wq(ev_w_uq[j])
            q, kn, krot, v = _mla_proj(h, pos, inv_row, ev_q_norm[j].reshape(1, -1),
                                       ev_kv_norm[j].reshape(1, -1), wqa, wqb,
                                       _pack_wkv(ev_w_ukv[j]))
            o_a = _attention(q, kn, krot, v, batch, seq)
            wg2 = jnp.concatenate(
                [ev_w_gk2[j], jnp.zeros((LANE - GLA_GATE_RANK, GLA_HEADS * GLA_DK), F32)], axis=0)
            wg2_hi = wg2.astype(BF16)
            wg2_lo = (wg2 - wg2_hi.astype(F32)).astype(BF16)
            o_b = _gla(h, wg2_hi, wg2_lo, ev_b_gk2[j].reshape(1, -1),
                       ev_gla_norm[j].reshape(1, -1), batch, seq)
            w_out = ev_w_out[j].astype(BF16)
            split = MLA_HEADS * MLA_V
            acts, ws = [o_a, o_b], [w_out[:split], w_out[split:]]
        else:
            h = _inproj(xf, od_w_in[j].astype(BF16), gelu=True)
            bs_full = jnp.repeat(od_b_s[j].T, LANE, axis=1)
            gated = _sgu(h, od_sgu_ln_g[j].reshape(1, -1), od_sgu_ln_b[j].reshape(1, -1),
                         od_w_s[j], bs_full)
            acts, ws = [gated], [od_w_out[j].astype(BF16)]
        x1, ri, rg, cnt = _outproj(acts, ws, xf, ln_mix_g[i].reshape(1, -1),
                                   ln_mix_b[i].reshape(1, -1), rw2, rw_hi, rb)
        xf = _moe(x1, ri, rg, cnt,
                  moe_w_gate[i].astype(BF16), moe_w_up[i].astype(BF16),
                  moe_w_down[i].astype(BF16),
                  ln_ffn_g[i].reshape(1, -1), ln_ffn_b[i].reshape(1, -1))
    return xf.reshape(batch, seq, d)
```

```python
import functools

import jax
import jax.numpy as jnp
import numpy as np
from jax import lax
from jax.experimental import pallas as pl
from jax.experimental.pallas import tpu as pltpu

F32 = jnp.float32
BF16 = jnp.bfloat16

D_MODEL = 2048
DEPTH = 2
CHUNK = 64
MLA_HEADS = 8
MLA_RANK = 512
MLA_NOPE = 128
MLA_ROPE = 64
MLA_V = 128
MLA_QK_PAD = 256
MLA_VT_ROWS = 144
LOG2_E = 1.4426950408889634
ROPE_THETA = 10000.0
GLA_HEADS = 4
GLA_DK = 128
GLA_DV = 256
GLA_GATE_RANK = 16
GLA_GATE_NORM = 16.0
SGU_BLOCK = 128
SGU_HALF = D_MODEL
SGU_GROUPS = 16
N_EXPERTS = 16
N_GROUPS = 4
EXPERTS_PER_GROUP = 4
D_EXPERT = 512
PAIR_SLOT_A = np.array([0, 0, 0, 1, 1, 3], np.int32)
PAIR_SLOT_B = np.array([1, 2, 3, 3, 2, 2], np.int32)
PAIRS_PER_GROUP = 6
N_CLASSES = N_GROUPS * PAIRS_PER_GROUP
CLASS_ROWS = 32
ROUTE_ROWS = 8
ALPHA = (2.0 * DEPTH) ** 0.25
EPS = 1e-6
NEG_BIG = -1e30

LANE = 128
VMEM_LIMIT = 56 * 1024 * 1024

COL_CQ, COL_CKV, COL_GQ, COL_GK = 0, 512, 1024, 1536
COL_GV, COL_GO, COL_KR, COL_GLR = 2048, 3072, 4096, 4224
EVEN_IN_PAD = 4352


def _dot(a, b):
    return jnp.dot(a, b, preferred_element_type=F32)


def _dot_nt(a, b):
    return lax.dot_general(a, b, (((1,), (1,)), ((), ())), preferred_element_type=F32)


def _dot_tn(a, b):
    return lax.dot_general(a, b, (((0,), (0,)), ((), ())), preferred_element_type=F32)


def _split_hi_lo(x):
    hi = x.astype(BF16)
    lo = (x - hi.astype(F32)).astype(BF16)
    return hi, lo


def _layer_norm(y, g, b):
    mu = jnp.mean(y, axis=-1, keepdims=True)
    d = y - mu
    var = jnp.mean(d * d, axis=-1, keepdims=True)
    return d * lax.rsqrt(var + EPS) * g + b


def _rms_norm(y, g):
    return y * lax.rsqrt(jnp.mean(y * y, axis=-1, keepdims=True) + EPS) * g


def _sigmoid(x):
    return 1.0 / (1.0 + jnp.exp(-x))


def _params(sem):
    return pltpu.CompilerParams(dimension_semantics=sem, vmem_limit_bytes=VMEM_LIMIT)


def _resident(shape):
    nd = len(shape)
    return pl.BlockSpec(shape, lambda *_: (0,) * nd, pipeline_mode=pl.Buffered(1))


def _inproj_kernel(x_ref, w_ref, o_ref, *, gelu, n_chunk):
    xb = x_ref[...].astype(BF16)
    n = w_ref.shape[1]
    for n0 in range(0, n, n_chunk):
        nw = min(n_chunk, n - n0)
        y = _dot(xb, w_ref[:, n0:n0 + nw])
        if gelu:
            y = 0.5 * y * (1.0 + jnp.tanh(0.7978845608028654 * (y + 0.044715 * (y * y * y))))
        o_ref[:, n0:n0 + nw] = y.astype(o_ref.dtype)


def _inproj(x, w, *, gelu, tm=512):
    t, d = x.shape
    n = w.shape[1]
    return pl.pallas_call(
        functools.partial(_inproj_kernel, gelu=gelu, n_chunk=512),
        out_shape=jax.ShapeDtypeStruct((t, n), BF16),
        grid=(t // tm,),
        in_specs=[pl.BlockSpec((tm, d), lambda i: (i, 0)), _resident((d, n))],
        out_specs=pl.BlockSpec((tm, n), lambda i: (i, 0)),
        compiler_params=_params(("parallel",)),
        name="inproj_gelu" if gelu else "inproj",
    )(x, w)


def _mla_proj_kernel(cq_ref, ckv_ref, kr_ref, pos_ref, inv_ref, gq_ref, gkv_ref,
                     wqa_ref, wqb_ref, wkn_ref, wvt_ref, q_ref, kn_ref, krot_ref, vt_ref):
    scale = (MLA_NOPE + MLA_ROPE) ** -0.5 * LOG2_E
    ang = pos_ref[...].astype(F32) * inv_ref[...]
    lane = lax.broadcasted_iota(jnp.int32, ang.shape, 1)
    rot_lane = lane < MLA_ROPE
    cos = jnp.where(rot_lane, jnp.cos(ang), 0.0)
    sin = jnp.where(rot_lane, jnp.sin(ang), 0.0)

    nq = _rms_norm(cq_ref[...].astype(F32), gq_ref[...]).astype(BF16)
    for h in range(MLA_HEADS):
        a = _dot(nq, wqa_ref[:, h * MLA_QK_PAD:(h + 1) * MLA_QK_PAD])
        b = _dot(nq, wqb_ref[:, h * LANE:(h + 1) * LANE])
        q_ref[:, h * MLA_QK_PAD:h * MLA_QK_PAD + LANE] = (a[:, :LANE] * scale).astype(BF16)
        q_ref[:, h * MLA_QK_PAD + LANE:(h + 1) * MLA_QK_PAD] = (
            (a[:, LANE:] * cos + b * sin) * scale).astype(BF16)

    nkv = _rms_norm(ckv_ref[...].astype(F32), gkv_ref[...]).astype(BF16)
    kn_ref[...] = _dot(nkv, wkn_ref[...]).astype(BF16)
    tm = nkv.shape[0]
    tail_row = lax.broadcasted_iota(jnp.int32, (MLA_VT_ROWS - MLA_V, tm), 0)
    tail = jnp.where(tail_row == 0, 1.0, 0.0).astype(BF16)
    for h in range(MLA_HEADS):
        vt = _dot_nt(wvt_ref[h * MLA_V:(h + 1) * MLA_V, :], nkv)
        vt_ref[h * MLA_VT_ROWS:h * MLA_VT_ROWS + MLA_V, :] = vt.astype(BF16)
        vt_ref[h * MLA_VT_ROWS + MLA_V:(h + 1) * MLA_VT_ROWS, :] = tail

    kr = kr_ref[...].astype(F32)
    krot_ref[...] = (kr * cos + pltpu.roll(kr, MLA_ROPE, axis=1) * sin).astype(BF16)


def _mla_proj(h, pos, inv, gq, gkv, wqa, wqb, wkn, wvt, *, tm=512):
    t = h.shape[0]
    hw = MLA_HEADS * MLA_NOPE
    return pl.pallas_call(
        _mla_proj_kernel,
        out_shape=(jax.ShapeDtypeStruct((t, MLA_HEADS * MLA_QK_PAD), BF16),
                   jax.ShapeDtypeStruct((t, hw), BF16),
                   jax.ShapeDtypeStruct((t, LANE), BF16),
                   jax.ShapeDtypeStruct((MLA_HEADS * MLA_VT_ROWS, t), BF16)),
        grid=(t // tm,),
        in_specs=[pl.BlockSpec((tm, MLA_RANK), lambda i: (i, COL_CQ // MLA_RANK)),
                  pl.BlockSpec((tm, MLA_RANK), lambda i: (i, COL_CKV // MLA_RANK)),
                  pl.BlockSpec((tm, LANE), lambda i: (i, COL_KR // LANE)),
                  pl.BlockSpec((tm, 1), lambda i: (i, 0)),
                  _resident((1, LANE)), _resident((1, MLA_RANK)), _resident((1, MLA_RANK)),
                  _resident(wqa.shape), _resident(wqb.shape), _resident(wkn.shape),
                  _resident(wvt.shape)],
        out_specs=(pl.BlockSpec((tm, MLA_HEADS * MLA_QK_PAD), lambda i: (i, 0)),
                   pl.BlockSpec((tm, hw), lambda i: (i, 0)),
                   pl.BlockSpec((tm, LANE), lambda i: (i, 0)),
                   pl.BlockSpec((MLA_HEADS * MLA_VT_ROWS, tm), lambda i: (0, i))),
        compiler_params=_params(("parallel",)),
        name="mla_proj",
    )(h, h, h, pos, inv, gq, gkv, wqa, wqb, wkn, wvt)


def _col_max(x):
    rows, cols = x.shape
    part = jnp.max(x.reshape(rows // 64, 64, cols), axis=0)
    return jnp.max(part, axis=0, keepdims=True)


def _attn_kernel(qi_ref, ki_ref, q_ref, kn_ref, krot_ref, vt_ref, o_ref, m_ref, acc_ref, *, tile):
    p_id = pl.program_id(1)
    qi = qi_ref[p_id]
    ki = ki_ref[p_id]

    @pl.when(ki == 0)
    def _():
        m_ref[...] = jnp.full(m_ref.shape, NEG_BIG, F32)
        acc_ref[...] = jnp.zeros(acc_ref.shape, F32)

    def step(masked):
        krot = krot_ref[...]
        if masked:
            k_chunk = lax.broadcasted_iota(jnp.int32, (tile, tile), 0) // CHUNK
            q_chunk = lax.broadcasted_iota(jnp.int32, (tile, tile), 1) // CHUNK
            visible = k_chunk <= q_chunk
        def scores(h):
            kh = jnp.concatenate([kn_ref[:, h * MLA_NOPE:(h + 1) * MLA_NOPE], krot], axis=1)
            return _dot_nt(kh, q_ref[:, h * MLA_QK_PAD:(h + 1) * MLA_QK_PAD])

        st_next = scores(0)
        for h in range(MLA_HEADS):
            st = st_next
            if h + 1 < MLA_HEADS:
                st_next = scores(h + 1)
            if masked:
                st = jnp.where(visible, st, NEG_BIG)
            m_prev = m_ref[h]
            m_new = jnp.maximum(m_prev, _col_max(st))
            alpha = jnp.exp2(m_prev - m_new)
            p = jnp.exp2((st - m_new).astype(BF16))
            m_ref[h] = m_new
            acc_ref[h] = alpha * acc_ref[h] + _dot(vt_ref[h * MLA_VT_ROWS:(h + 1) * MLA_VT_ROWS, :], p)

    @pl.when(ki != qi)
    def _():
        step(False)

    @pl.when(ki == qi)
    def _():
        step(True)
        for h in range(MLA_HEADS):
            acc = acc_ref[h]
            out_t = acc[:MLA_V, :] / acc[MLA_V:MLA_V + 1, :]
            o_ref[:, h * MLA_V:(h + 1) * MLA_V] = out_t.T.astype(o_ref.dtype)


def _attn_pairs(n_tiles):
    qi, ki = [], []
    for i in range(n_tiles):
        for j in range(i + 1):
            qi.append(i)
            ki.append(j)
    return np.asarray(qi, np.int32), np.asarray(ki, np.int32)


def _attention(q, kn, krot, vt, batch, seq, *, tile=512):
    qi, ki = _attn_pairs(seq // tile)
    n = seq // tile
    hw = MLA_HEADS * MLA_V
    grid_spec = pltpu.PrefetchScalarGridSpec(
        num_scalar_prefetch=2,
        grid=(batch, len(qi)),
        in_specs=[
            pl.BlockSpec((tile, MLA_HEADS * MLA_QK_PAD), lambda b, p, qi, ki: (b * n + qi[p], 0)),
            pl.BlockSpec((tile, hw), lambda b, p, qi, ki: (b * n + ki[p], 0)),
            pl.BlockSpec((tile, LANE), lambda b, p, qi, ki: (b * n + ki[p], 0)),
            pl.BlockSpec((MLA_HEADS * MLA_VT_ROWS, tile), lambda b, p, qi, ki: (0, b * n + ki[p])),
        ],
        out_specs=pl.BlockSpec((tile, hw), lambda b, p, qi, ki: (b * n + qi[p], 0)),
        scratch_shapes=[pltpu.VMEM((MLA_HEADS, 1, tile), F32),
                        pltpu.VMEM((MLA_HEADS, MLA_VT_ROWS, tile), F32)],
    )
    return pl.pallas_call(
        functools.partial(_attn_kernel, tile=tile),
        out_shape=jax.ShapeDtypeStruct((batch * seq, hw), BF16),
        grid_spec=grid_spec,
        compiler_params=_params(("parallel", "arbitrary")),
        name="mla_attention",
    )(jnp.asarray(qi), jnp.asarray(ki), q, kn, krot, vt)


def _gla_kernel(gq_ref, gk_ref, gv_ref, go_ref, glr_ref, whi_ref, wlo_ref, bg_ref, gn_ref,
                o_ref, st_ref, *, n_chunks):
    @pl.when(pl.program_id(1) == 0)
    def _():
        st_ref[...] = jnp.zeros(st_ref.shape, F32)

    row = lax.broadcasted_iota(jnp.int32, (CHUNK, CHUNK), 0)
    col = lax.broadcasted_iota(jnp.int32, (CHUNK, CHUNK), 1)
    tril = jnp.where(col <= row, 1.0, 0.0).astype(BF16)
    q_scale = GLA_DK ** -0.5

    def chunk_body(c, carry):
        rows = pl.ds(pl.multiple_of(c * CHUNK, CHUNK), CHUNK)
        glr = glr_ref[rows, :]
        logit = _dot(glr, whi_ref[...]) + _dot(glr, wlo_ref[...]) + bg_ref[...]
        lg = -(jnp.maximum(-logit, 0.0) + jnp.log1p(jnp.exp(-jnp.abs(logit)))) / GLA_GATE_NORM
        lg_hi, lg_lo = _split_hi_lo(lg)
        cum = _dot(tril, lg_hi) + _dot(tril, lg_lo)
        cum_end = cum[CHUNK - 1:CHUNK, :]
        k_dec = (gk_ref[rows, :].astype(F32) * jnp.exp(cum_end - cum)).astype(BF16)
        decay = jnp.exp(cum_end)
        qs = (gq_ref[rows, :].astype(F32) * q_scale).astype(BF16)
        for h in range(GLA_HEADS):
            ks = slice(h * GLA_DK, (h + 1) * GLA_DK)
            vs = slice(h * GLA_DV, (h + 1) * GLA_DV)
            upd_t = _dot_tn(gv_ref[rows, vs], k_dec[:, ks])
            st = st_ref[h] * decay[:, ks] + upd_t
            st_ref[h] = st
            o = _dot_nt(qs[:, ks], st.astype(BF16))
            o = _rms_norm(o, gn_ref[...])
            g = go_ref[rows, vs].astype(F32)
            o_ref[rows, vs] = (o * (g * _sigmoid(g))).astype(o_ref.dtype)
        return carry

    lax.fori_loop(0, n_chunks, chunk_body, 0)


def _gla(h, whi, wlo, bg, gn, batch, seq, *, tc=512):
    t = h.shape[0]
    nb = seq // tc
    kw, vw = GLA_HEADS * GLA_DK, GLA_HEADS * GLA_DV
    return pl.pallas_call(
        functools.partial(_gla_kernel, n_chunks=tc // CHUNK),
        out_shape=jax.ShapeDtypeStruct((t, vw), BF16),
        grid=(batch, nb),
        in_specs=[pl.BlockSpec((tc, kw), lambda b, i: (b * nb + i, COL_GQ // kw)),
                  pl.BlockSpec((tc, kw), lambda b, i: (b * nb + i, COL_GK // kw)),
                  pl.BlockSpec((tc, vw), lambda b, i: (b * nb + i, COL_GV // vw)),
                  pl.BlockSpec((tc, vw), lambda b, i: (b * nb + i, COL_GO // vw)),
                  pl.BlockSpec((tc, LANE), lambda b, i: (b * nb + i, COL_GLR // LANE)),
                  _resident(whi.shape), _resident(wlo.shape), _resident(bg.shape),
                  _resident(gn.shape)],
        out_specs=pl.BlockSpec((tc, vw), lambda b, i: (b * nb + i, 0)),
        scratch_shapes=[pltpu.VMEM((GLA_HEADS, GLA_DV, GLA_DK), F32)],
        compiler_params=_params(("parallel", "arbitrary")),
        name="gla",
    )(h, h, h, h, h, whi, wlo, bg, gn)


def _sgu_kernel(u_ref, v_ref, lg_ref, lb_ref, ws_ref, bs_ref, o_ref):
    tm = u_ref.shape[0]
    vn = _layer_norm(v_ref[...].astype(F32), lg_ref[...], lb_ref[...]).astype(BF16)
    pos_chunk_i = lax.broadcasted_iota(jnp.int32, (SGU_BLOCK, SGU_BLOCK), 0) // CHUNK
    pos_chunk_j = lax.broadcasted_iota(jnp.int32, (SGU_BLOCK, SGU_BLOCK), 1) // CHUNK
    causal = pos_chunk_j <= pos_chunk_i
    for g in range(SGU_GROUPS):
        cs = slice(g * LANE, (g + 1) * LANE)
        w = jnp.where(causal, ws_ref[g], 0.0).astype(BF16)
        for n in range(tm // SGU_BLOCK):
            rs = slice(n * SGU_BLOCK, (n + 1) * SGU_BLOCK)
            mixed = _dot(w, vn[rs, cs]) + bs_ref[:, cs]
            o_ref[rs, cs] = (u_ref[rs, cs].astype(F32) * mixed).astype(o_ref.dtype)


def _sgu(h, lg, lb, ws, bs_full, *, tm=256):
    t = h.shape[0]
    return pl.pallas_call(
        _sgu_kernel,
        out_shape=jax.ShapeDtypeStruct((t, SGU_HALF), BF16),
        grid=(t // tm,),
        in_specs=[pl.BlockSpec((tm, SGU_HALF), lambda i: (i, 0)),
                  pl.BlockSpec((tm, SGU_HALF), lambda i: (i, 1)),
                  _resident(lg.shape), _resident(lb.shape), _resident(ws.shape),
                  _resident(bs_full.shape)],
        out_specs=pl.BlockSpec((tm, SGU_HALF), lambda i: (i, 0)),
        compiler_params=_params(("parallel",)),
        name="sgu",
    )(h, h, lg, lb, ws, bs_full)


def _route(x1, rw2_ref, rwhi_ref, rb_ref):
    x_hi, x_lo = _split_hi_lo(x1)
    part = _dot_nt(rw2_ref[...], x_hi)
    logits = part[:N_EXPERTS] + part[N_EXPERTS:] + _dot_nt(rwhi_ref[...], x_lo)
    scores = _sigmoid(logits)
    biased = scores + rb_ref[...]
    b = [biased[e:e + 1, :] for e in range(N_EXPERTS)]
    s = [scores[e:e + 1, :] for e in range(N_EXPERTS)]

    def top2_sum(v):
        best = v[0] + v[1]
        for i in range(len(v)):
            for j in range(i + 1, len(v)):
                if (i, j) != (0, 1):
                    best = jnp.maximum(best, v[i] + v[j])
        return best

    def argmax_first(v):
        best_v, best_i = v[0], jnp.zeros(v[0].shape, jnp.int32)
        for i in range(1, len(v)):
            upd = v[i] > best_v
            best_i = jnp.where(upd, i, best_i)
            best_v = jnp.where(upd, v[i], best_v)
        return best_i

    def pick(v, idx):
        out = v[len(v) - 1]
        for i in range(len(v) - 2, -1, -1):
            out = jnp.where(idx == i, v[i], out)
        return out

    grp = [top2_sum(b[g * EXPERTS_PER_GROUP:(g + 1) * EXPERTS_PER_GROUP]) for g in range(N_GROUPS)]
    best_g = argmax_first(grp)
    cand_b = [pick([b[g * EXPERTS_PER_GROUP + j] for g in range(N_GROUPS)], best_g)
              for j in range(EXPERTS_PER_GROUP)]
    cand_s = [pick([s[g * EXPERTS_PER_GROUP + j] for g in range(N_GROUPS)], best_g)
              for j in range(EXPERTS_PER_GROUP)]
    i1 = argmax_first(cand_b)
    cand_b2 = [jnp.where(i1 == j, -jnp.inf, cand_b[j]) for j in range(EXPERTS_PER_GROUP)]
    i2 = argmax_first(cand_b2)
    g1 = pick(cand_s, i1)
    g2 = pick(cand_s, i2)
    den = g1 + g2
    w1, w2 = g1 / den, g2 / den
    lo, hi = jnp.minimum(i1, i2), jnp.maximum(i1, i2)
    pair = jnp.where(lo == 0, hi - 1, jnp.where(lo == 1, jnp.where(hi == 3, 3, 4), 5))
    slot_a = jnp.where(pair < 3, 0, jnp.where(pair < 5, 1, 3))
    first_is_a = i1 == slot_a
    cls = best_g * PAIRS_PER_GROUP + pair
    return cls, jnp.where(first_is_a, w1, w2), jnp.where(first_is_a, w2, w1)


def _outproj_kernel(*refs, n_in):
    a_refs = refs[:n_in]
    w_refs = refs[n_in:2 * n_in]
    (x_ref, g_ref, b_ref, rw2_ref, rwhi_ref, rb_ref,
     x1_ref, ri_ref, rg_ref, cnt_ref, carry_ref) = refs[2 * n_in:]
    tm = x_ref.shape[0]

    @pl.when(pl.program_id(0) == 0)
    def _():
        carry_ref[...] = jnp.zeros(carry_ref.shape, F32)

    mix = _dot(a_refs[0][...], w_refs[0][...])
    for a_ref, w_ref in zip(a_refs[1:], w_refs[1:]):
        mix = mix + _dot(a_ref[...], w_ref[...])
    x1 = _layer_norm(ALPHA * x_ref[...] + mix, g_ref[...], b_ref[...])
    x1_ref[...] = x1

    cls, gate_a, gate_b = _route(x1, rw2_ref, rwhi_ref, rb_ref)
    class_row = lax.broadcasted_iota(jnp.int32, (CLASS_ROWS, tm), 0)
    onehot = jnp.where(class_row == cls, 1.0, 0.0)
    t_prev = lax.broadcasted_iota(jnp.int32, (tm, tm), 0)
    t_cur = lax.broadcasted_iota(jnp.int32, (tm, tm), 1)
    before = jnp.where(t_prev < t_cur, 1.0, 0.0).astype(BF16)
    seen = _dot(onehot.astype(BF16), before) + carry_ref[...]
    rank = jnp.sum(onehot * seen, axis=0, keepdims=True)
    carry_ref[...] += jnp.sum(onehot, axis=1, keepdims=True)

    pad_i = jnp.zeros((ROUTE_ROWS - 2, tm), jnp.int32)
    ri_ref[...] = jnp.concatenate([cls, rank.astype(jnp.int32), pad_i], axis=0)
    rg_ref[...] = jnp.concatenate([gate_a, gate_b, pad_i.astype(F32)], axis=0)
    cnt_ref[...] = jnp.broadcast_to(carry_ref[...], cnt_ref.shape).astype(jnp.int32)


def _outproj(acts, ws, x, g, b, rw2, rwhi, rb, *, tm=256):
    t, d = x.shape
    n_in = len(acts)
    in_specs = ([pl.BlockSpec((tm, a.shape[1]), lambda i: (i, 0)) for a in acts]
                + [_resident(w.shape) for w in ws]
                + [pl.BlockSpec((tm, d), lambda i: (i, 0)),
                   _resident(g.shape), _resident(b.shape), _resident(rw2.shape),
                   _resident(rwhi.shape), _resident(rb.shape)])
    return pl.pallas_call(
        functools.partial(_outproj_kernel, n_in=n_in),
        out_shape=(jax.ShapeDtypeStruct((t, d), F32),
                   jax.ShapeDtypeStruct((ROUTE_ROWS, t), jnp.int32),
                   jax.ShapeDtypeStruct((ROUTE_ROWS, t), F32),
                   jax.ShapeDtypeStruct((CLASS_ROWS, LANE), jnp.int32)),
        grid=(t // tm,),
        in_specs=in_specs,
        out_specs=(pl.BlockSpec((tm, d), lambda i: (i, 0)),
                   pl.BlockSpec((ROUTE_ROWS, tm), lambda i: (0, i)),
                   pl.BlockSpec((ROUTE_ROWS, tm), lambda i: (0, i)),
                   pl.BlockSpec((CLASS_ROWS, LANE), lambda i: (0, 0))),
        scratch_shapes=[pltpu.VMEM((CLASS_ROWS, 1), F32)],
        compiler_params=_params(("arbitrary",)),
        name="outproj_ln_route",
    )(*acts, *ws, x, g, b, rw2, rwhi, rb)


def _moe_plan(ri, rg, cnt, t, tm, n_tiles):
    cls, rank = ri[0], ri[1]
    counts = cnt[:N_CLASSES, 0]
    tiles_c = (counts + tm - 1) // tm
    tile_end = jnp.cumsum(tiles_c)
    tile_start = tile_end - tiles_c
    used = tile_end[-1]
    slot = tile_start[cls] * tm + rank
    per_token = jnp.stack([jnp.arange(t, dtype=F32), rg[0], rg[1]], axis=1)
    per_slot = jnp.zeros((n_tiles * tm, 3), F32).at[slot].set(per_token, unique_indices=True)
    inv = per_slot[:, 0].astype(jnp.int32)
    gates = per_slot[:, 1:]
    tile = jnp.arange(n_tiles, dtype=jnp.int32)
    covered = tile_end[None, :] <= jnp.minimum(tile, used - 1)[:, None]
    tile_cls = jnp.minimum(jnp.sum(covered.astype(jnp.int32), axis=1), N_CLASSES - 1)
    rows = jnp.clip(counts[tile_cls] - (tile - tile_start[tile_cls]) * tm, 0, tm)
    rows = jnp.where(tile < used, rows, 0).astype(jnp.int32)
    grp, pair = tile_cls // PAIRS_PER_GROUP, tile_cls % PAIRS_PER_GROUP
    expert_a = (grp * EXPERTS_PER_GROUP + jnp.asarray(PAIR_SLOT_A)[pair]).astype(jnp.int32)
    expert_b = (grp * EXPERTS_PER_GROUP + jnp.asarray(PAIR_SLOT_B)[pair]).astype(jnp.int32)
    return inv, gates, expert_a, expert_b, rows


def _moe_kernel(inv_ref, ea_ref, eb_ref, rows_ref, x_hbm, gs_ref,
                wga_ref, wua_ref, wda_ref, wgb_ref, wub_ref, wdb_ref, g_ref, b_ref,
                o_hbm, xbuf, obuf, gsem, ssem, *, tm, n_tiles):
    i = pl.program_id(0)
    cur = i % 2

    def row_copy_in(tile, r, buf):
        tok = inv_ref[tile * tm + r]
        return pltpu.make_async_copy(x_hbm.at[pl.ds(tok, 1), :],
                                     xbuf.at[buf, pl.ds(r, 1), :], gsem.at[buf])

    def row_copy_out(tile, r, buf):
        tok = inv_ref[tile * tm + r]
        return pltpu.make_async_copy(obuf.at[buf, pl.ds(r, 1), :],
                                     o_hbm.at[pl.ds(tok, 1), :], ssem.at[buf])

    def start_gather(tile, buf):
        def body(r, carry):
            row_copy_in(tile, r, buf).start()
            return carry
        lax.fori_loop(0, tm, body, 0, unroll=8)

    @pl.when(jnp.logical_and(i == 0, rows_ref[0] > 0))
    def _():
        start_gather(0, 0)

    nxt = jnp.minimum(i + 1, n_tiles - 1)

    @pl.when(jnp.logical_and(i + 1 < n_tiles, rows_ref[nxt] > 0))
    def _():
        start_gather(i + 1, 1 - cur)

    def wait_scatter(n_rows, buf):
        bit = 1
        while bit <= tm:
            @pl.when((n_rows & bit) != 0)
            def _(bit=bit):
                pltpu.make_async_copy(obuf.at[buf, pl.ds(0, bit), :],
                                      o_hbm.at[pl.ds(0, bit), :], ssem.at[buf]).wait()
            bit *= 2

    prev2 = jnp.maximum(i - 2, 0)

    @pl.when(jnp.logical_and(i >= 2, rows_ref[prev2] > 0))
    def _():
        wait_scatter(rows_ref[prev2], cur)

    n_rows = rows_ref[i]

    @pl.when(n_rows > 0)
    def _():
        pltpu.make_async_copy(x_hbm.at[pl.ds(0, tm), :], xbuf.at[cur], gsem.at[cur]).wait()
        x = xbuf[cur]
        xb = x.astype(BF16)
        gates = gs_ref[...]

        def expert(wg_ref, wu_ref, gate_col):
            gate = _dot(xb, wg_ref[0])
            up = _dot(xb, wu_ref[0])
            return ((gate * _sigmoid(gate)) * up * gate_col).astype(BF16)

        y = (_dot(expert(wga_ref, wua_ref, gates[:, 0:1]), wda_ref[0])
             + _dot(expert(wgb_ref, wub_ref, gates[:, 1:2]), wdb_ref[0]))
        obuf[cur] = _layer_norm(ALPHA * x + y, g_ref[...], b_ref[...])

        def body(r, carry):
            row_copy_out(i, r, cur).start()
            return carry
        lax.fori_loop(0, n_rows, body, 0)


def _moe(x, ri, rg, cnt, wg, wu, wd, g, b, *, tm=256):
    t, d = x.shape
    n_tiles = t // tm + N_CLASSES + 2
    inv, gates, expert_a, expert_b, rows = _moe_plan(ri, rg, cnt, t, tm, n_tiles)
    w_in = lambda sel: pl.BlockSpec((1, d, D_EXPERT), lambda i, inv, ea, eb, rows: (sel(ea, eb)[i], 0, 0))
    w_out = lambda sel: pl.BlockSpec((1, D_EXPERT, d), lambda i, inv, ea, eb, rows: (sel(ea, eb)[i], 0, 0))
    pick_a = lambda ea, eb: ea
    pick_b = lambda ea, eb: eb
    grid_spec = pltpu.PrefetchScalarGridSpec(
        num_scalar_prefetch=4,
        grid=(n_tiles,),
        in_specs=[pl.BlockSpec(memory_space=pl.ANY),
                  pl.BlockSpec((tm, 2), lambda i, *_: (i, 0)),
                  w_in(pick_a), w_in(pick_a), w_out(pick_a),
                  w_in(pick_b), w_in(pick_b), w_out(pick_b),
                  pl.BlockSpec(g.shape, lambda i, *_: (0, 0)),
                  pl.BlockSpec(b.shape, lambda i, *_: (0, 0))],
        out_specs=pl.BlockSpec(memory_space=pl.ANY),
        scratch_shapes=[pltpu.VMEM((2, tm, d), F32), pltpu.VMEM((2, tm, d), F32),
                        pltpu.SemaphoreType.DMA((2,)), pltpu.SemaphoreType.DMA((2,))],
    )
    return pl.pallas_call(
        functools.partial(_moe_kernel, tm=tm, n_tiles=n_tiles),
        out_shape=jax.ShapeDtypeStruct((t, d), F32),
        grid_spec=grid_spec,
        compiler_params=_params(("arbitrary",)),
        name="moe_ln",
    )(inv, expert_a, expert_b, rows, x, gates, wg, wu, wd, wg, wu, wd, g, b)


def _rotate_half_cols(w):
    half = MLA_ROPE // 2
    return jnp.concatenate([-w[..., half:], w[..., :half]], axis=-1)


def _pack_even_in(w_in):
    cq, ckv, kr, gq, gk, gv, glr, go = jnp.split(
        w_in, np.cumsum([512, 512, 64, 512, 512, 1024, 16])[:].tolist(), axis=1)
    pad = jnp.zeros((w_in.shape[0], EVEN_IN_PAD - COL_GLR - GLA_GATE_RANK), w_in.dtype)
    return jnp.concatenate([cq, ckv, gq, gk, gv, go, kr, _rotate_half_cols(kr), glr, pad],
                           axis=1).astype(BF16)


def _pack_wq(w_uq):
    r = w_uq.shape[0]
    nope, rope = w_uq[..., :MLA_NOPE], w_uq[..., MLA_NOPE:]
    zeros = jnp.zeros((r, MLA_HEADS, MLA_QK_PAD - MLA_NOPE - MLA_ROPE), w_uq.dtype)
    wqa = jnp.concatenate([nope, rope, zeros], axis=-1).reshape(r, MLA_HEADS * MLA_QK_PAD)
    wqb = jnp.concatenate([_rotate_half_cols(rope), zeros], axis=-1).reshape(r, MLA_HEADS * LANE)
    return wqa.astype(BF16), wqb.astype(BF16)


def _pack_wkv(w_ukv):
    r = w_ukv.shape[0]
    kn = w_ukv[..., :MLA_NOPE].reshape(r, MLA_HEADS * MLA_NOPE)
    v = w_ukv[..., MLA_NOPE:].reshape(r, MLA_HEADS * MLA_V)
    return kn.astype(BF16), v.T.astype(BF16)


def kernel(x, positions, ev_w_in, ev_q_norm, ev_kv_norm, ev_w_uq, ev_w_ukv, ev_w_gk2, ev_b_gk2, ev_gla_norm, ev_w_out, od_w_in, od_sgu_ln_g, od_sgu_ln_b, od_w_s, od_b_s, od_w_out, ln_mix_g, ln_mix_b, ln_ffn_g, ln_ffn_b, router_w, router_b, moe_w_gate, moe_w_up, moe_w_down):
    batch, seq, d = x.shape
    t = batch * seq
    xf = x.reshape(t, d)
    pos = positions.reshape(t, 1)

    inv = 1.0 / (ROPE_THETA ** (jnp.arange(0, MLA_ROPE, 2, dtype=F32) / MLA_ROPE))
    inv_row = jnp.concatenate([inv, inv, jnp.zeros((LANE - MLA_ROPE,), F32)]).reshape(1, LANE)

    rw_t = router_w.T
    rw_hi = rw_t.astype(BF16)
    rw_lo = (rw_t - rw_hi.astype(F32)).astype(BF16)
    rw2 = jnp.concatenate([rw_hi, rw_lo], axis=0)
    rb = router_b.astype(F32).reshape(N_EXPERTS, 1)

    for i in range(DEPTH):
        j = i // 2
        if i % 2 == 0:
            h = _inproj(xf, _pack_even_in(ev_w_in[j]), gelu=False)
            wqa, wqb = _pack_wq(ev_w_uq[j])
            wkn, wvt = _pack_wkv(ev_w_ukv[j])
            q, kn, krot, vt = _mla_proj(h, pos, inv_row, ev_q_norm[j].reshape(1, -1),
                                        ev_kv_norm[j].reshape(1, -1), wqa, wqb, wkn, wvt)
            o_a = _attention(q, kn, krot, vt, batch, seq)
            wg2 = jnp.concatenate(
                [ev_w_gk2[j], jnp.zeros((LANE - GLA_GATE_RANK, GLA_HEADS * GLA_DK), F32)], axis=0)
            wg2_hi = wg2.astype(BF16)
            wg2_lo = (wg2 - wg2_hi.astype(F32)).astype(BF16)
            o_b = _gla(h, wg2_hi, wg2_lo, ev_b_gk2[j].reshape(1, -1),
                       ev_gla_norm[j].reshape(1, -1), batch, seq)
            w_out = ev_w_out[j].astype(BF16)
            split = MLA_HEADS * MLA_V
            acts, ws = [o_a, o_b], [w_out[:split], w_out[split:]]
        else:
            h = _inproj(xf, od_w_in[j].astype(BF16), gelu=True)
            bs_full = jnp.repeat(od_b_s[j].T, LANE, axis=1)
            gated = _sgu(h, od_sgu_ln_g[j].reshape(1, -1), od_sgu_ln_b[j].reshape(1, -1),
                         od_w_s[j], bs_full)
            acts, ws = [gated], [od_w_out[j].astype(BF16)]
        x1, ri, rg, cnt = _outproj(acts, ws, xf, ln_mix_g[i].reshape(1, -1),
                                   ln_mix_b[i].reshape(1, -1), rw2, rw_hi, rb)
        xf = _moe(x1, ri, rg, cnt,
                  moe_w_gate[i].astype(BF16), moe_w_up[i].astype(BF16),
                  moe_w_down[i].astype(BF16),
                  ln_ffn_g[i].reshape(1, -1), ln_ffn_b[i].reshape(1, -1))
    return xf.reshape(batch, seq, d)
```

```python
import functools

import jax
import jax.numpy as jnp
import numpy as np
from jax import lax
from jax.experimental import pallas as pl
from jax.experimental.pallas import tpu as pltpu

F32 = jnp.float32
BF16 = jnp.bfloat16

D_MODEL = 2048
DEPTH = 2
CHUNK = 64
MLA_HEADS = 8
MLA_RANK = 512
MLA_NOPE = 128
MLA_ROPE = 64
MLA_V = 128
MLA_QK_PAD = 256
MLA_VT_ROWS = 144
LOG2_E = 1.4426950408889634
ROPE_THETA = 10000.0
GLA_HEADS = 4
GLA_DK = 128
GLA_DV = 256
GLA_GATE_RANK = 16
GLA_GATE_NORM = 16.0
SGU_BLOCK = 128
SGU_HALF = D_MODEL
SGU_GROUPS = 16
N_EXPERTS = 16
N_GROUPS = 4
EXPERTS_PER_GROUP = 4
D_EXPERT = 512
PAIR_SLOT_A = np.array([0, 0, 0, 1, 1, 3], np.int32)
PAIR_SLOT_B = np.array([1, 2, 3, 3, 2, 2], np.int32)
PAIRS_PER_GROUP = 6
N_CLASSES = N_GROUPS * PAIRS_PER_GROUP
ROW_BITS = 24
CLASS_ROWS = 32
ROUTE_ROWS = 8
ALPHA = (2.0 * DEPTH) ** 0.25
EPS = 1e-6
NEG_BIG = -1e30

LANE = 128
VMEM_LIMIT = 56 * 1024 * 1024

COL_CQ, COL_CKV, COL_GQ, COL_GK = 0, 512, 1024, 1536
COL_GV, COL_GO, COL_KR, COL_GLR = 2048, 3072, 4096, 4224
EVEN_IN_PAD = 4352


def _dot(a, b):
    return jnp.dot(a, b, preferred_element_type=F32)


def _dot_nt(a, b):
    return lax.dot_general(a, b, (((1,), (1,)), ((), ())), preferred_element_type=F32)


def _dot_tn(a, b):
    return lax.dot_general(a, b, (((0,), (0,)), ((), ())), preferred_element_type=F32)


def _split_hi_lo(x):
    hi = x.astype(BF16)
    lo = (x - hi.astype(F32)).astype(BF16)
    return hi, lo


def _layer_norm(y, g, b):
    mu = jnp.mean(y, axis=-1, keepdims=True)
    d = y - mu
    var = jnp.mean(d * d, axis=-1, keepdims=True)
    return d * lax.rsqrt(var + EPS) * g + b


def _rms_norm(y, g):
    return y * lax.rsqrt(jnp.mean(y * y, axis=-1, keepdims=True) + EPS) * g


def _sigmoid(x):
    return 1.0 / (1.0 + jnp.exp(-x))


def _params(sem):
    return pltpu.CompilerParams(dimension_semantics=sem, vmem_limit_bytes=VMEM_LIMIT)


def _resident(shape):
    nd = len(shape)
    return pl.BlockSpec(shape, lambda *_: (0,) * nd, pipeline_mode=pl.Buffered(1))


def _inproj_kernel(x_ref, w_ref, o_ref, *, gelu, n_chunk):
    xb = x_ref[...].astype(BF16)
    n = w_ref.shape[1]
    for n0 in range(0, n, n_chunk):
        nw = min(n_chunk, n - n0)
        y = _dot(xb, w_ref[:, n0:n0 + nw])
        if gelu:
            y = 0.5 * y * (1.0 + jnp.tanh(0.7978845608028654 * (y + 0.044715 * (y * y * y))))
        o_ref[:, n0:n0 + nw] = y.astype(o_ref.dtype)


def _inproj(x, w, t, *, gelu, tm=512):
    d = x.shape[1]
    n = w.shape[1]
    return pl.pallas_call(
        functools.partial(_inproj_kernel, gelu=gelu, n_chunk=512),
        out_shape=jax.ShapeDtypeStruct((t, n), BF16),
        grid=(t // tm,),
        in_specs=[pl.BlockSpec((tm, d), lambda i: (i, 0)), _resident((d, n))],
        out_specs=pl.BlockSpec((tm, n), lambda i: (i, 0)),
        compiler_params=_params(("parallel",)),
        name="inproj_gelu" if gelu else "inproj",
    )(x, w)


def _mla_proj_kernel(cq_ref, ckv_ref, kr_ref, pos_ref, inv_ref, gq_ref, gkv_ref,
                     wqa_ref, wqb_ref, wkn_ref, wvt_ref, q_ref, kn_ref, krot_ref, vt_ref):
    scale = (MLA_NOPE + MLA_ROPE) ** -0.5 * LOG2_E
    ang = pos_ref[...].astype(F32) * inv_ref[...]
    lane = lax.broadcasted_iota(jnp.int32, ang.shape, 1)
    rot_lane = lane < MLA_ROPE
    cos = jnp.where(rot_lane, jnp.cos(ang), 0.0)
    sin = jnp.where(rot_lane, jnp.sin(ang), 0.0)

    nq = _rms_norm(cq_ref[...].astype(F32), gq_ref[...]).astype(BF16)
    for h in range(MLA_HEADS):
        a = _dot(nq, wqa_ref[:, h * MLA_QK_PAD:(h + 1) * MLA_QK_PAD])
        b = _dot(nq, wqb_ref[:, h * LANE:(h + 1) * LANE])
        q_ref[:, h * MLA_QK_PAD:h * MLA_QK_PAD + LANE] = (a[:, :LANE] * scale).astype(BF16)
        q_ref[:, h * MLA_QK_PAD + LANE:(h + 1) * MLA_QK_PAD] = (
            (a[:, LANE:] * cos + b * sin) * scale).astype(BF16)

    nkv = _rms_norm(ckv_ref[...].astype(F32), gkv_ref[...]).astype(BF16)
    kn_ref[...] = _dot(nkv, wkn_ref[...]).astype(BF16)
    tm = nkv.shape[0]
    tail_row = lax.broadcasted_iota(jnp.int32, (MLA_VT_ROWS - MLA_V, tm), 0)
    tail = jnp.where(tail_row == 0, 1.0, 0.0).astype(BF16)
    for h in range(MLA_HEADS):
        vt = _dot_nt(wvt_ref[h * MLA_V:(h + 1) * MLA_V, :], nkv)
        vt_ref[h * MLA_VT_ROWS:h * MLA_VT_ROWS + MLA_V, :] = vt.astype(BF16)
        vt_ref[h * MLA_VT_ROWS + MLA_V:(h + 1) * MLA_VT_ROWS, :] = tail

    kr = kr_ref[...].astype(F32)
    krot_ref[...] = (kr * cos + pltpu.roll(kr, MLA_ROPE, axis=1) * sin).astype(BF16)


def _mla_proj(h, pos, inv, gq, gkv, wqa, wqb, wkn, wvt, *, tm=512):
    t = h.shape[0]
    hw = MLA_HEADS * MLA_NOPE
    return pl.pallas_call(
        _mla_proj_kernel,
        out_shape=(jax.ShapeDtypeStruct((t, MLA_HEADS * MLA_QK_PAD), BF16),
                   jax.ShapeDtypeStruct((t, hw), BF16),
                   jax.ShapeDtypeStruct((t, LANE), BF16),
                   jax.ShapeDtypeStruct((MLA_HEADS * MLA_VT_ROWS, t), BF16)),
        grid=(t // tm,),
        in_specs=[pl.BlockSpec((tm, MLA_RANK), lambda i: (i, COL_CQ // MLA_RANK)),
                  pl.BlockSpec((tm, MLA_RANK), lambda i: (i, COL_CKV // MLA_RANK)),
                  pl.BlockSpec((tm, LANE), lambda i: (i, COL_KR // LANE)),
                  pl.BlockSpec((tm, 1), lambda i: (i, 0)),
                  _resident((1, LANE)), _resident((1, MLA_RANK)), _resident((1, MLA_RANK)),
                  _resident(wqa.shape), _resident(wqb.shape), _resident(wkn.shape),
                  _resident(wvt.shape)],
        out_specs=(pl.BlockSpec((tm, MLA_HEADS * MLA_QK_PAD), lambda i: (i, 0)),
                   pl.BlockSpec((tm, hw), lambda i: (i, 0)),
                   pl.BlockSpec((tm, LANE), lambda i: (i, 0)),
                   pl.BlockSpec((MLA_HEADS * MLA_VT_ROWS, tm), lambda i: (0, i))),
        compiler_params=_params(("parallel",)),
        name="mla_proj",
    )(h, h, h, pos, inv, gq, gkv, wqa, wqb, wkn, wvt)


def _col_max(x):
    rows, cols = x.shape
    part = jnp.max(x.reshape(rows // 64, 64, cols), axis=0)
    return jnp.max(part, axis=0, keepdims=True)


def _attn_kernel(qi_ref, ki_ref, q_ref, kn_ref, krot_ref, vt_ref, o_ref, m_ref, acc_ref, *, tile):
    p_id = pl.program_id(1)
    qi = qi_ref[p_id]
    ki = ki_ref[p_id]

    @pl.when(ki == 0)
    def _():
        m_ref[...] = jnp.full(m_ref.shape, NEG_BIG, F32)
        acc_ref[...] = jnp.zeros(acc_ref.shape, F32)

    def step(masked):
        krot = krot_ref[...]
        if masked:
            k_chunk = lax.broadcasted_iota(jnp.int32, (tile, tile), 0) // CHUNK
            q_chunk = lax.broadcasted_iota(jnp.int32, (tile, tile), 1) // CHUNK
            visible = k_chunk <= q_chunk
        def scores(h):
            kh = jnp.concatenate([kn_ref[:, h * MLA_NOPE:(h + 1) * MLA_NOPE], krot], axis=1)
            return _dot_nt(kh, q_ref[:, h * MLA_QK_PAD:(h + 1) * MLA_QK_PAD])

        st_next = scores(0)
        for h in range(MLA_HEADS):
            st = st_next
            if h + 1 < MLA_HEADS:
                st_next = scores(h + 1)
            if masked:
                st = jnp.where(visible, st, NEG_BIG)
            m_prev = m_ref[h]
            m_new = jnp.maximum(m_prev, _col_max(st))
            alpha = jnp.exp2(m_prev - m_new)
            p = jnp.exp2((st - m_new).astype(BF16))
            m_ref[h] = m_new
            acc_ref[h] = alpha * acc_ref[h] + _dot(vt_ref[h * MLA_VT_ROWS:(h + 1) * MLA_VT_ROWS, :], p)

    @pl.when(ki != qi)
    def _():
        step(False)

    @pl.when(ki == qi)
    def _():
        step(True)
        for h in range(MLA_HEADS):
            acc = acc_ref[h]
            out_t = acc[:MLA_V, :] / acc[MLA_V:MLA_V + 1, :]
            o_ref[:, h * MLA_V:(h + 1) * MLA_V] = out_t.T.astype(o_ref.dtype)


def _attn_pairs(n_tiles):
    qi, ki = [], []
    for i in range(n_tiles):
        for j in range(i + 1):
            qi.append(i)
            ki.append(j)
    return np.asarray(qi, np.int32), np.asarray(ki, np.int32)


def _attention(q, kn, krot, vt, batch, seq, *, tile=512):
    qi, ki = _attn_pairs(seq // tile)
    n = seq // tile
    hw = MLA_HEADS * MLA_V
    grid_spec = pltpu.PrefetchScalarGridSpec(
        num_scalar_prefetch=2,
        grid=(batch, len(qi)),
        in_specs=[
            pl.BlockSpec((tile, MLA_HEADS * MLA_QK_PAD), lambda b, p, qi, ki: (b * n + qi[p], 0)),
            pl.BlockSpec((tile, hw), lambda b, p, qi, ki: (b * n + ki[p], 0)),
            pl.BlockSpec((tile, LANE), lambda b, p, qi, ki: (b * n + ki[p], 0)),
            pl.BlockSpec((MLA_HEADS * MLA_VT_ROWS, tile), lambda b, p, qi, ki: (0, b * n + ki[p])),
        ],
        out_specs=pl.BlockSpec((tile, hw), lambda b, p, qi, ki: (b * n + qi[p], 0)),
        scratch_shapes=[pltpu.VMEM((MLA_HEADS, 1, tile), F32),
                        pltpu.VMEM((MLA_HEADS, MLA_VT_ROWS, tile), F32)],
    )
    return pl.pallas_call(
        functools.partial(_attn_kernel, tile=tile),
        out_shape=jax.ShapeDtypeStruct((batch * seq, hw), BF16),
        grid_spec=grid_spec,
        compiler_params=_params(("parallel", "arbitrary")),
        name="mla_attention",
    )(jnp.asarray(qi), jnp.asarray(ki), q, kn, krot, vt)


def _gla_kernel(gq_ref, gk_ref, gv_ref, go_ref, glr_ref, whi_ref, wlo_ref, bg_ref, gn_ref,
                o_ref, st_ref, *, n_chunks):
    @pl.when(pl.program_id(1) == 0)
    def _():
        st_ref[...] = jnp.zeros(st_ref.shape, F32)

    row = lax.broadcasted_iota(jnp.int32, (CHUNK, CHUNK), 0)
    col = lax.broadcasted_iota(jnp.int32, (CHUNK, CHUNK), 1)
    tril = jnp.where(col <= row, 1.0, 0.0).astype(BF16)
    q_scale = GLA_DK ** -0.5

    def chunk_body(c, carry):
        rows = pl.ds(pl.multiple_of(c * CHUNK, CHUNK), CHUNK)
        glr = glr_ref[rows, :]
        logit = _dot(glr, whi_ref[...]) + _dot(glr, wlo_ref[...]) + bg_ref[...]
        lg = -(jnp.maximum(-logit, 0.0) + jnp.log1p(jnp.exp(-jnp.abs(logit)))) / GLA_GATE_NORM
        lg_hi, lg_lo = _split_hi_lo(lg)
        cum = _dot(tril, lg_hi) + _dot(tril, lg_lo)
        cum_end = cum[CHUNK - 1:CHUNK, :]
        k_dec = (gk_ref[rows, :].astype(F32) * jnp.exp(cum_end - cum)).astype(BF16)
        decay = jnp.exp(cum_end)
        qs = (gq_ref[rows, :].astype(F32) * q_scale).astype(BF16)
        for h in range(GLA_HEADS):
            ks = slice(h * GLA_DK, (h + 1) * GLA_DK)
            vs = slice(h * GLA_DV, (h + 1) * GLA_DV)
            upd_t = _dot_tn(gv_ref[rows, vs], k_dec[:, ks])
            st = st_ref[h] * decay[:, ks] + upd_t
            st_ref[h] = st
            o = _dot_nt(qs[:, ks], st.astype(BF16))
            o = _rms_norm(o, gn_ref[...])
            g = go_ref[rows, vs].astype(F32)
            o_ref[rows, vs] = (o * (g * _sigmoid(g))).astype(o_ref.dtype)
        return carry

    lax.fori_loop(0, n_chunks, chunk_body, 0)


def _gla(h, whi, wlo, bg, gn, batch, seq, *, tc=512):
    t = h.shape[0]
    nb = seq // tc
    kw, vw = GLA_HEADS * GLA_DK, GLA_HEADS * GLA_DV
    return pl.pallas_call(
        functools.partial(_gla_kernel, n_chunks=tc // CHUNK),
        out_shape=jax.ShapeDtypeStruct((t, vw), BF16),
        grid=(batch, nb),
        in_specs=[pl.BlockSpec((tc, kw), lambda b, i: (b * nb + i, COL_GQ // kw)),
                  pl.BlockSpec((tc, kw), lambda b, i: (b * nb + i, COL_GK // kw)),
                  pl.BlockSpec((tc, vw), lambda b, i: (b * nb + i, COL_GV // vw)),
                  pl.BlockSpec((tc, vw), lambda b, i: (b * nb + i, COL_GO // vw)),
                  pl.BlockSpec((tc, LANE), lambda b, i: (b * nb + i, COL_GLR // LANE)),
                  _resident(whi.shape), _resident(wlo.shape), _resident(bg.shape),
                  _resident(gn.shape)],
        out_specs=pl.BlockSpec((tc, vw), lambda b, i: (b * nb + i, 0)),
        scratch_shapes=[pltpu.VMEM((GLA_HEADS, GLA_DV, GLA_DK), F32)],
        compiler_params=_params(("parallel", "arbitrary")),
        name="gla",
    )(h, h, h, h, h, whi, wlo, bg, gn)


def _sgu_kernel(u_ref, v_ref, lg_ref, lb_ref, ws_ref, bs_ref, o_ref):
    tm = u_ref.shape[0]
    vn = _layer_norm(v_ref[...].astype(F32), lg_ref[...], lb_ref[...]).astype(BF16)
    pos_chunk_i = lax.broadcasted_iota(jnp.int32, (SGU_BLOCK, SGU_BLOCK), 0) // CHUNK
    pos_chunk_j = lax.broadcasted_iota(jnp.int32, (SGU_BLOCK, SGU_BLOCK), 1) // CHUNK
    causal = pos_chunk_j <= pos_chunk_i
    for g in range(SGU_GROUPS):
        cs = slice(g * LANE, (g + 1) * LANE)
        w = jnp.where(causal, ws_ref[g], 0.0).astype(BF16)
        for n in range(tm // SGU_BLOCK):
            rs = slice(n * SGU_BLOCK, (n + 1) * SGU_BLOCK)
            mixed = _dot(w, vn[rs, cs]) + bs_ref[:, cs]
            o_ref[rs, cs] = (u_ref[rs, cs].astype(F32) * mixed).astype(o_ref.dtype)


def _sgu(h, lg, lb, ws, bs_full, *, tm=256):
    t = h.shape[0]
    return pl.pallas_call(
        _sgu_kernel,
        out_shape=jax.ShapeDtypeStruct((t, SGU_HALF), BF16),
        grid=(t // tm,),
        in_specs=[pl.BlockSpec((tm, SGU_HALF), lambda i: (i, 0)),
                  pl.BlockSpec((tm, SGU_HALF), lambda i: (i, 1)),
                  _resident(lg.shape), _resident(lb.shape), _resident(ws.shape),
                  _resident(bs_full.shape)],
        out_specs=pl.BlockSpec((tm, SGU_HALF), lambda i: (i, 0)),
        compiler_params=_params(("parallel",)),
        name="sgu",
    )(h, h, lg, lb, ws, bs_full)


def _route(x1, rw2_ref, rwhi_ref, rb_ref):
    x_hi, x_lo = _split_hi_lo(x1)
    part = _dot_nt(rw2_ref[...], x_hi)
    logits = part[:N_EXPERTS] + part[N_EXPERTS:] + _dot_nt(rwhi_ref[...], x_lo)
    scores = _sigmoid(logits)
    biased = scores + rb_ref[...]
    b = [biased[e:e + 1, :] for e in range(N_EXPERTS)]
    s = [scores[e:e + 1, :] for e in range(N_EXPERTS)]

    def top2_sum(v):
        best = v[0] + v[1]
        for i in range(len(v)):
            for j in range(i + 1, len(v)):
                if (i, j) != (0, 1):
                    best = jnp.maximum(best, v[i] + v[j])
        return best

    def argmax_first(v):
        best_v, best_i = v[0], jnp.zeros(v[0].shape, jnp.int32)
        for i in range(1, len(v)):
            upd = v[i] > best_v
            best_i = jnp.where(upd, i, best_i)
            best_v = jnp.where(upd, v[i], best_v)
        return best_i

    def pick(v, idx):
        out = v[len(v) - 1]
        for i in range(len(v) - 2, -1, -1):
            out = jnp.where(idx == i, v[i], out)
        return out

    grp = [top2_sum(b[g * EXPERTS_PER_GROUP:(g + 1) * EXPERTS_PER_GROUP]) for g in range(N_GROUPS)]
    best_g = argmax_first(grp)
    cand_b = [pick([b[g * EXPERTS_PER_GROUP + j] for g in range(N_GROUPS)], best_g)
              for j in range(EXPERTS_PER_GROUP)]
    cand_s = [pick([s[g * EXPERTS_PER_GROUP + j] for g in range(N_GROUPS)], best_g)
              for j in range(EXPERTS_PER_GROUP)]
    i1 = argmax_first(cand_b)
    cand_b2 = [jnp.where(i1 == j, -jnp.inf, cand_b[j]) for j in range(EXPERTS_PER_GROUP)]
    i2 = argmax_first(cand_b2)
    g1 = pick(cand_s, i1)
    g2 = pick(cand_s, i2)
    den = g1 + g2
    w1, w2 = g1 / den, g2 / den
    lo, hi = jnp.minimum(i1, i2), jnp.maximum(i1, i2)
    pair = jnp.where(lo == 0, hi - 1, jnp.where(lo == 1, jnp.where(hi == 3, 3, 4), 5))
    slot_a = jnp.where(pair < 3, 0, jnp.where(pair < 5, 1, 3))
    first_is_a = i1 == slot_a
    cls = best_g * PAIRS_PER_GROUP + pair
    return cls, jnp.where(first_is_a, w1, w2), jnp.where(first_is_a, w2, w1)


def _outproj_kernel(*refs, n_in):
    a_refs = refs[:n_in]
    w_refs = refs[n_in:2 * n_in]
    (x_ref, g_ref, b_ref, rw2_ref, rwhi_ref, rb_ref,
     x1_ref, ri_ref, rg_ref, cnt_ref, carry_ref) = refs[2 * n_in:]
    tm = x_ref.shape[0]

    @pl.when(pl.program_id(0) == 0)
    def _():
        carry_ref[...] = jnp.zeros(carry_ref.shape, F32)

    mix = _dot(a_refs[0][...], w_refs[0][...])
    for a_ref, w_ref in zip(a_refs[1:], w_refs[1:]):
        mix = mix + _dot(a_ref[...], w_ref[...])
    x1 = _layer_norm(ALPHA * x_ref[...] + mix, g_ref[...], b_ref[...])
    x1_ref[...] = x1

    cls, gate_a, gate_b = _route(x1, rw2_ref, rwhi_ref, rb_ref)
    class_row = lax.broadcasted_iota(jnp.int32, (CLASS_ROWS, tm), 0)
    onehot = jnp.where(class_row == cls, 1.0, 0.0)
    t_prev = lax.broadcasted_iota(jnp.int32, (tm, tm), 0)
    t_cur = lax.broadcasted_iota(jnp.int32, (tm, tm), 1)
    before = jnp.where(t_prev < t_cur, 1.0, 0.0).astype(BF16)
    seen = _dot(onehot.astype(BF16), before) + carry_ref[...]
    rank = jnp.sum(onehot * seen, axis=0, keepdims=True)
    carry_ref[...] += jnp.sum(onehot, axis=1, keepdims=True)

    pad_i = jnp.zeros((ROUTE_ROWS - 2, tm), jnp.int32)
    ri_ref[...] = jnp.concatenate([cls, rank.astype(jnp.int32), pad_i], axis=0)
    rg_ref[...] = jnp.concatenate([gate_a, gate_b, pad_i.astype(F32)], axis=0)
    cnt_ref[...] = jnp.broadcast_to(carry_ref[...], cnt_ref.shape).astype(jnp.int32)


def _outproj(acts, ws, x, g, b, rw2, rwhi, rb, *, tm=256):
    t, d = acts[0].shape[0], x.shape[1]
    n_in = len(acts)
    in_specs = ([pl.BlockSpec((tm, a.shape[1]), lambda i: (i, 0)) for a in acts]
                + [_resident(w.shape) for w in ws]
                + [pl.BlockSpec((tm, d), lambda i: (i, 0)),
                   _resident(g.shape), _resident(b.shape), _resident(rw2.shape),
                   _resident(rwhi.shape), _resident(rb.shape)])
    return pl.pallas_call(
        functools.partial(_outproj_kernel, n_in=n_in),
        out_shape=(jax.ShapeDtypeStruct((t, d), F32),
                   jax.ShapeDtypeStruct((ROUTE_ROWS, t), jnp.int32),
                   jax.ShapeDtypeStruct((ROUTE_ROWS, t), F32),
                   jax.ShapeDtypeStruct((CLASS_ROWS, LANE), jnp.int32)),
        grid=(t // tm,),
        in_specs=in_specs,
        out_specs=(pl.BlockSpec((tm, d), lambda i: (i, 0)),
                   pl.BlockSpec((ROUTE_ROWS, tm), lambda i: (0, i)),
                   pl.BlockSpec((ROUTE_ROWS, tm), lambda i: (0, i)),
                   pl.BlockSpec((CLASS_ROWS, LANE), lambda i: (0, 0))),
        scratch_shapes=[pltpu.VMEM((CLASS_ROWS, 1), F32)],
        compiler_params=_params(("arbitrary",)),
        name="outproj_ln_route",
    )(*acts, *ws, x, g, b, rw2, rwhi, rb)


def _moe_plan(ri, rg, cnt, t, tm, n_tiles):
    cls, rank = ri[0], ri[1]
    counts = cnt[:N_CLASSES, 0]
    tiles_c = (counts + tm - 1) // tm
    tile_end = jnp.cumsum(tiles_c)
    tile_start = tile_end - tiles_c
    used = tile_end[-1]
    slot = tile_start[cls] * tm + rank
    per_token = jnp.stack([jnp.arange(1, t + 1, dtype=F32), rg[0], rg[1]], axis=1)
    per_slot = jnp.zeros((n_tiles * tm, 3), F32).at[slot].set(per_token, unique_indices=True)
    gates = per_slot[:, 1:]
    token = per_slot[:, 0].astype(jnp.int32) - 1
    occupied = token >= 0
    slot_id = jnp.arange(n_tiles * tm, dtype=jnp.int32)
    src_row = jnp.where(occupied, token, 0)
    dst_row = jnp.where(occupied, token, t + ((slot_id // tm) % 2) * tm + slot_id % tm)
    tile = jnp.arange(n_tiles, dtype=jnp.int32)
    covered = tile_end[None, :] <= jnp.minimum(tile, used - 1)[:, None]
    tile_cls = jnp.minimum(jnp.sum(covered.astype(jnp.int32), axis=1), N_CLASSES - 1)
    rows = jnp.clip(counts[tile_cls] - (tile - tile_start[tile_cls]) * tm, 0, tm)
    rows = jnp.where(tile < used, rows, 0).astype(jnp.int32)
    grp, pair = tile_cls // PAIRS_PER_GROUP, tile_cls % PAIRS_PER_GROUP
    expert_a = (grp * EXPERTS_PER_GROUP + jnp.asarray(PAIR_SLOT_A)[pair]).astype(jnp.int32)
    expert_b = (grp * EXPERTS_PER_GROUP + jnp.asarray(PAIR_SLOT_B)[pair]).astype(jnp.int32)
    return src_row, dst_row, gates, expert_a, expert_b, rows


def _moe_kernel(src_ref, dst_ref, ea_ref, eb_ref, rows_ref, x_hbm, gs_ref,
                wga_ref, wua_ref, wda_ref, wgb_ref, wub_ref, wdb_ref, g_ref, b_ref,
                o_hbm, xbuf, obuf, xb_ref, y_ref, gsem, ssem, *, tm, n_tiles, n_rows_out):
    i = pl.program_id(0)
    cur = i % 2
    other = 1 - cur

    def used_at(tile):
        inside = jnp.logical_and(tile >= 0, tile < n_tiles)
        return jnp.logical_and(inside, rows_ref[jnp.clip(tile, 0, n_tiles - 1)] > 0)

    used_prev2, used_prev, used_cur, used_next = (used_at(i - 2), used_at(i - 1), used_at(i),
                                                  used_at(i + 1))

    def gather_row(tile, r, buf, skew=0):
        row = src_ref[tile * tm + r + skew]
        pltpu.make_async_copy(x_hbm.at[pl.ds(row, 1), :],
                              xbuf.at[buf, pl.ds(r, 1), :], gsem.at[buf]).start()
        return row

    def scatter_row(tile, r, buf, skew=0):
        row = dst_ref[tile * tm + r + skew]
        pltpu.make_async_copy(obuf.at[buf, pl.ds(r, 1), :],
                              o_hbm.at[pl.ds(row, 1), :], ssem.at[buf]).start()
        return row

    def gather_loop(tile, buf):
        def body(r, carry):
            gather_row(tile, r, buf)
            return carry
        lax.fori_loop(0, tm, body, 0, unroll=8)

    def scatter_loop(tile, buf):
        def body(r, carry):
            scatter_row(tile, r, buf)
            return carry
        lax.fori_loop(0, tm, body, 0, unroll=8)

    def ffn():
        xb = xb_ref[...]
        gates = gs_ref[...]

        def expert(wg_ref, wu_ref, gate_col):
            gate = _dot(xb, wg_ref[0])
            up = _dot(xb, wu_ref[0])
            return ((gate * _sigmoid(gate)) * up * gate_col).astype(BF16)

        y_ref[...] = (_dot(expert(wga_ref, wua_ref, gates[:, 0:1]), wda_ref[0])
                      + _dot(expert(wgb_ref, wub_ref, gates[:, 1:2]), wdb_ref[0]))

    @pl.when(i == 0)
    def _():
        obuf[...] = jnp.zeros(obuf.shape, F32)
        for half in range(2):
            fill = pltpu.make_async_copy(obuf.at[half], o_hbm.at[pl.ds(n_rows_out + half * tm, tm), :],
                                         ssem.at[half])
            fill.start()
            fill.wait()

    @pl.when(jnp.logical_and(i == 0, used_cur))
    def _():
        gather_loop(0, 0)

    @pl.when(used_cur)
    def _():
        pltpu.make_async_copy(x_hbm.at[pl.ds(0, tm), :], xbuf.at[cur], gsem.at[cur]).wait()
        xb_ref[...] = xbuf[cur].astype(BF16)

    steady = jnp.logical_and(jnp.logical_and(used_prev, used_cur), used_next)

    @pl.when(steady)
    def _():
        skew = rows_ref[i] >> ROW_BITS
        for r in range(tm):
            skew = gather_row(i + 1, r, other, skew) >> ROW_BITS
            skew = scatter_row(i - 1, r, other, skew) >> ROW_BITS
        ffn()

    @pl.when(jnp.logical_not(steady))
    def _():
        @pl.when(used_next)
        def _():
            gather_loop(i + 1, other)

        @pl.when(used_prev)
        def _():
            scatter_loop(i - 1, other)

        @pl.when(used_cur)
        def _():
            ffn()

    @pl.when(used_prev2)
    def _():
        pltpu.make_async_copy(obuf.at[cur], o_hbm.at[pl.ds(0, tm), :], ssem.at[cur]).wait()

    @pl.when(used_cur)
    def _():
        obuf[cur] = _layer_norm(ALPHA * xbuf[cur] + y_ref[...], g_ref[...], b_ref[...])


def _moe(x, ri, rg, cnt, wg, wu, wd, g, b, *, tm=256):
    t, d = x.shape
    n_tiles = t // tm + N_CLASSES + 2
    src_row, dst_row, gates, expert_a, expert_b, rows = _moe_plan(ri, rg, cnt, t, tm, n_tiles)
    w_in = lambda sel: pl.BlockSpec((1, d, D_EXPERT), lambda i, s, dr, ea, eb, rows: (sel(ea, eb)[i], 0, 0))
    w_out = lambda sel: pl.BlockSpec((1, D_EXPERT, d), lambda i, s, dr, ea, eb, rows: (sel(ea, eb)[i], 0, 0))
    pick_a = lambda ea, eb: ea
    pick_b = lambda ea, eb: eb
    grid_spec = pltpu.PrefetchScalarGridSpec(
        num_scalar_prefetch=5,
        grid=(n_tiles,),
        in_specs=[pl.BlockSpec(memory_space=pl.ANY),
                  pl.BlockSpec((tm, 2), lambda i, *_: (i, 0)),
                  w_in(pick_a), w_in(pick_a), w_out(pick_a),
                  w_in(pick_b), w_in(pick_b), w_out(pick_b),
                  pl.BlockSpec(g.shape, lambda i, *_: (0, 0)),
                  pl.BlockSpec(b.shape, lambda i, *_: (0, 0))],
        out_specs=pl.BlockSpec(memory_space=pl.ANY),
        scratch_shapes=[pltpu.VMEM((2, tm, d), F32), pltpu.VMEM((2, tm, d), F32),
                        pltpu.VMEM((tm, d), BF16), pltpu.VMEM((tm, d), F32),
                        pltpu.SemaphoreType.DMA((2,)), pltpu.SemaphoreType.DMA((2,))],
    )
    return pl.pallas_call(
        functools.partial(_moe_kernel, tm=tm, n_tiles=n_tiles, n_rows_out=t),
        out_shape=jax.ShapeDtypeStruct((t + 2 * tm, d), F32),
        grid_spec=grid_spec,
        compiler_params=_params(("arbitrary",)),
        name="moe_ln",
    )(src_row, dst_row, expert_a, expert_b, rows, x, gates, wg, wu, wd, wg, wu, wd, g, b)


def _rotate_half_cols(w):
    half = MLA_ROPE // 2
    return jnp.concatenate([-w[..., half:], w[..., :half]], axis=-1)


def _pack_even_in(w_in):
    cq, ckv, kr, gq, gk, gv, glr, go = jnp.split(
        w_in, np.cumsum([512, 512, 64, 512, 512, 1024, 16])[:].tolist(), axis=1)
    pad = jnp.zeros((w_in.shape[0], EVEN_IN_PAD - COL_GLR - GLA_GATE_RANK), w_in.dtype)
    return jnp.concatenate([cq, ckv, gq, gk, gv, go, kr, _rotate_half_cols(kr), glr, pad],
                           axis=1).astype(BF16)


def _pack_wq(w_uq):
    r = w_uq.shape[0]
    nope, rope = w_uq[..., :MLA_NOPE], w_uq[..., MLA_NOPE:]
    zeros = jnp.zeros((r, MLA_HEADS, MLA_QK_PAD - MLA_NOPE - MLA_ROPE), w_uq.dtype)
    wqa = jnp.concatenate([nope, rope, zeros], axis=-1).reshape(r, MLA_HEADS * MLA_QK_PAD)
    wqb = jnp.concatenate([_rotate_half_cols(rope), zeros], axis=-1).reshape(r, MLA_HEADS * LANE)
    return wqa.astype(BF16), wqb.astype(BF16)


def _pack_wkv(w_ukv):
    r = w_ukv.shape[0]
    kn = w_ukv[..., :MLA_NOPE].reshape(r, MLA_HEADS * MLA_NOPE)
    v = w_ukv[..., MLA_NOPE:].reshape(r, MLA_HEADS * MLA_V)
    return kn.astype(BF16), v.T.astype(BF16)


def kernel(x, positions, ev_w_in, ev_q_norm, ev_kv_norm, ev_w_uq, ev_w_ukv, ev_w_gk2, ev_b_gk2, ev_gla_norm, ev_w_out, od_w_in, od_sgu_ln_g, od_sgu_ln_b, od_w_s, od_b_s, od_w_out, ln_mix_g, ln_mix_b, ln_ffn_g, ln_ffn_b, router_w, router_b, moe_w_gate, moe_w_up, moe_w_down):
    batch, seq, d = x.shape
    t = batch * seq
    xf = x.reshape(t, d)
    pos = positions.reshape(t, 1)

    inv = 1.0 / (ROPE_THETA ** (jnp.arange(0, MLA_ROPE, 2, dtype=F32) / MLA_ROPE))
    inv_row = jnp.concatenate([inv, inv, jnp.zeros((LANE - MLA_ROPE,), F32)]).reshape(1, LANE)

    rw_t = router_w.T
    rw_hi = rw_t.astype(BF16)
    rw_lo = (rw_t - rw_hi.astype(F32)).astype(BF16)
    rw2 = jnp.concatenate([rw_hi, rw_lo], axis=0)
    rb = router_b.astype(F32).reshape(N_EXPERTS, 1)

    for i in range(DEPTH):
        j = i // 2
        if i % 2 == 0:
            h = _inproj(xf, _pack_even_in(ev_w_in[j]), t, gelu=False)
            wqa, wqb = _pack_wq(ev_w_uq[j])
            wkn, wvt = _pack_wkv(ev_w_ukv[j])
            q, kn, krot, vt = _mla_proj(h, pos, inv_row, ev_q_norm[j].reshape(1, -1),
                                        ev_kv_norm[j].reshape(1, -1), wqa, wqb, wkn, wvt)
            o_a = _attention(q, kn, krot, vt, batch, seq)
            wg2 = jnp.concatenate(
                [ev_w_gk2[j], jnp.zeros((LANE - GLA_GATE_RANK, GLA_HEADS * GLA_DK), F32)], axis=0)
            wg2_hi = wg2.astype(BF16)
            wg2_lo = (wg2 - wg2_hi.astype(F32)).astype(BF16)
            o_b = _gla(h, wg2_hi, wg2_lo, ev_b_gk2[j].reshape(1, -1),
                       ev_gla_norm[j].reshape(1, -1), batch, seq)
            w_out = ev_w_out[j].astype(BF16)
            split = MLA_HEADS * MLA_V
            acts, ws = [o_a, o_b], [w_out[:split], w_out[split:]]
        else:
            h = _inproj(xf, od_w_in[j].astype(BF16), t, gelu=True)
            bs_full = jnp.repeat(od_b_s[j].T, LANE, axis=1)
            gated = _sgu(h, od_sgu_ln_g[j].reshape(1, -1), od_sgu_ln_b[j].reshape(1, -1),
                         od_w_s[j], bs_full)
            acts, ws = [gated], [od_w_out[j].astype(BF16)]
        x1, ri, rg, cnt = _outproj(acts, ws, xf, ln_mix_g[i].reshape(1, -1),
                                   ln_mix_b[i].reshape(1, -1), rw2, rw_hi, rb)
        xf = _moe(x1, ri, rg, cnt,
                  moe_w_gate[i].astype(BF16), moe_w_up[i].astype(BF16),
                  moe_w_down[i].astype(BF16),
                  ln_ffn_g[i].reshape(1, -1), ln_ffn_b[i].reshape(1, -1))
    return xf[:t].reshape(batch, seq, d)
```

```python
import functools

import jax
import jax.numpy as jnp
import numpy as np
from jax import lax
from jax.experimental import pallas as pl
from jax.experimental.pallas import tpu as pltpu

F32 = jnp.float32
BF16 = jnp.bfloat16

D_MODEL = 2048
DEPTH = 2
CHUNK = 64
MLA_HEADS = 8
MLA_RANK = 512
MLA_NOPE = 128
MLA_ROPE = 64
MLA_V = 128
MLA_QK_PAD = 256
MLA_VT_ROWS = 144
LOG2_E = 1.4426950408889634
ROPE_THETA = 10000.0
GLA_HEADS = 4
GLA_DK = 128
GLA_DV = 256
GLA_GATE_RANK = 16
GLA_GATE_NORM = 16.0
SGU_BLOCK = 128
SGU_HALF = D_MODEL
SGU_GROUPS = 16
N_EXPERTS = 16
N_GROUPS = 4
EXPERTS_PER_GROUP = 4
D_EXPERT = 512
PAIR_SLOT_A = np.array([0, 0, 0, 1, 1, 3], np.int32)
PAIR_SLOT_B = np.array([1, 2, 3, 3, 2, 2], np.int32)
PAIRS_PER_GROUP = 6
N_CLASSES = N_GROUPS * PAIRS_PER_GROUP
ROW_BITS = 24
CLASS_ROWS = 32
ROUTE_ROWS = 8
ALPHA = (2.0 * DEPTH) ** 0.25
EPS = 1e-6
NEG_BIG = -1e30

LANE = 128
VMEM_LIMIT = 56 * 1024 * 1024

COL_CQ, COL_CKV, COL_GQ, COL_GK = 0, 512, 1024, 1536
COL_GV, COL_GO, COL_KR, COL_GLR = 2048, 3072, 4096, 4224
EVEN_IN_PAD = 4352


def _dot(a, b):
    return jnp.dot(a, b, preferred_element_type=F32)


def _dot_nt(a, b):
    return lax.dot_general(a, b, (((1,), (1,)), ((), ())), preferred_element_type=F32)


def _dot_tn(a, b):
    return lax.dot_general(a, b, (((0,), (0,)), ((), ())), preferred_element_type=F32)


def _split_hi_lo(x):
    hi = x.astype(BF16)
    lo = (x - hi.astype(F32)).astype(BF16)
    return hi, lo


def _layer_norm(y, g, b):
    mu = jnp.mean(y, axis=-1, keepdims=True)
    d = y - mu
    var = jnp.mean(d * d, axis=-1, keepdims=True)
    return d * lax.rsqrt(var + EPS) * g + b


def _rms_norm(y, g):
    return y * lax.rsqrt(jnp.mean(y * y, axis=-1, keepdims=True) + EPS) * g


def _sigmoid(x):
    return 1.0 / (1.0 + jnp.exp(-x))


def _params(sem):
    return pltpu.CompilerParams(dimension_semantics=sem, vmem_limit_bytes=VMEM_LIMIT)


def _resident(shape):
    nd = len(shape)
    return pl.BlockSpec(shape, lambda *_: (0,) * nd, pipeline_mode=pl.Buffered(1))


def _inproj_kernel(x_ref, w_ref, o_ref, *, gelu, n_chunk):
    xb = x_ref[...].astype(BF16)
    n = w_ref.shape[1]
    for n0 in range(0, n, n_chunk):
        nw = min(n_chunk, n - n0)
        y = _dot(xb, w_ref[:, n0:n0 + nw])
        if gelu:
            y = 0.5 * y * (1.0 + jnp.tanh(0.7978845608028654 * (y + 0.044715 * (y * y * y))))
        o_ref[:, n0:n0 + nw] = y.astype(o_ref.dtype)


def _inproj(x, w, t, *, gelu, tm=512):
    d = x.shape[1]
    n = w.shape[1]
    return pl.pallas_call(
        functools.partial(_inproj_kernel, gelu=gelu, n_chunk=512),
        out_shape=jax.ShapeDtypeStruct((t, n), BF16),
        grid=(t // tm,),
        in_specs=[pl.BlockSpec((tm, d), lambda i: (i, 0)), _resident((d, n))],
        out_specs=pl.BlockSpec((tm, n), lambda i: (i, 0)),
        compiler_params=_params(("parallel",)),
        name="inproj_gelu" if gelu else "inproj",
    )(x, w)


def _mla_proj_kernel(cq_ref, ckv_ref, kr_ref, pos_ref, inv_ref, gq_ref, gkv_ref,
                     wqa_ref, wqb_ref, wkn_ref, wvt_ref, q_ref, kn_ref, krot_ref, vt_ref):
    scale = (MLA_NOPE + MLA_ROPE) ** -0.5 * LOG2_E
    ang = pos_ref[...].astype(F32) * inv_ref[...]
    lane = lax.broadcasted_iota(jnp.int32, ang.shape, 1)
    rot_lane = lane < MLA_ROPE
    cos = jnp.where(rot_lane, jnp.cos(ang), 0.0)
    sin = jnp.where(rot_lane, jnp.sin(ang), 0.0)

    nq = _rms_norm(cq_ref[...].astype(F32), gq_ref[...]).astype(BF16)
    for h in range(MLA_HEADS):
        a = _dot(nq, wqa_ref[:, h * MLA_QK_PAD:(h + 1) * MLA_QK_PAD])
        b = _dot(nq, wqb_ref[:, h * LANE:(h + 1) * LANE])
        q_ref[:, h * MLA_QK_PAD:h * MLA_QK_PAD + LANE] = (a[:, :LANE] * scale).astype(BF16)
        q_ref[:, h * MLA_QK_PAD + LANE:(h + 1) * MLA_QK_PAD] = (
            (a[:, LANE:] * cos + b * sin) * scale).astype(BF16)

    nkv = _rms_norm(ckv_ref[...].astype(F32), gkv_ref[...]).astype(BF16)
    kn_ref[...] = _dot(nkv, wkn_ref[...]).astype(BF16)
    tm = nkv.shape[0]
    tail_row = lax.broadcasted_iota(jnp.int32, (MLA_VT_ROWS - MLA_V, tm), 0)
    tail = jnp.where(tail_row == 0, 1.0, 0.0).astype(BF16)
    for h in range(MLA_HEADS):
        vt = _dot_nt(wvt_ref[h * MLA_V:(h + 1) * MLA_V, :], nkv)
        vt_ref[h * MLA_VT_ROWS:h * MLA_VT_ROWS + MLA_V, :] = vt.astype(BF16)
        vt_ref[h * MLA_VT_ROWS + MLA_V:(h + 1) * MLA_VT_ROWS, :] = tail

    kr = kr_ref[...].astype(F32)
    krot_ref[...] = (kr * cos + pltpu.roll(kr, MLA_ROPE, axis=1) * sin).astype(BF16)


def _mla_proj(h, pos, inv, gq, gkv, wqa, wqb, wkn, wvt, *, tm=512):
    t = h.shape[0]
    hw = MLA_HEADS * MLA_NOPE
    return pl.pallas_call(
        _mla_proj_kernel,
        out_shape=(jax.ShapeDtypeStruct((t, MLA_HEADS * MLA_QK_PAD), BF16),
                   jax.ShapeDtypeStruct((t, hw), BF16),
                   jax.ShapeDtypeStruct((t, LANE), BF16),
                   jax.ShapeDtypeStruct((MLA_HEADS * MLA_VT_ROWS, t), BF16)),
        grid=(t // tm,),
        in_specs=[pl.BlockSpec((tm, MLA_RANK), lambda i: (i, COL_CQ // MLA_RANK)),
                  pl.BlockSpec((tm, MLA_RANK), lambda i: (i, COL_CKV // MLA_RANK)),
                  pl.BlockSpec((tm, LANE), lambda i: (i, COL_KR // LANE)),
                  pl.BlockSpec((tm, 1), lambda i: (i, 0)),
                  _resident((1, LANE)), _resident((1, MLA_RANK)), _resident((1, MLA_RANK)),
                  _resident(wqa.shape), _resident(wqb.shape), _resident(wkn.shape),
                  _resident(wvt.shape)],
        out_specs=(pl.BlockSpec((tm, MLA_HEADS * MLA_QK_PAD), lambda i: (i, 0)),
                   pl.BlockSpec((tm, hw), lambda i: (i, 0)),
                   pl.BlockSpec((tm, LANE), lambda i: (i, 0)),
                   pl.BlockSpec((MLA_HEADS * MLA_VT_ROWS, tm), lambda i: (0, i))),
        compiler_params=_params(("parallel",)),
        name="mla_proj",
    )(h, h, h, pos, inv, gq, gkv, wqa, wqb, wkn, wvt)


def _col_max(x):
    rows, cols = x.shape
    part = jnp.max(x.reshape(rows // 64, 64, cols), axis=0)
    return jnp.max(part, axis=0, keepdims=True)


def _attn_kernel(qi_ref, ki_ref, q_ref, kn_ref, krot_ref, vt_ref, o_ref, m_ref, acc_ref, *, tile):
    p_id = pl.program_id(1)
    qi = qi_ref[p_id]
    ki = ki_ref[p_id]

    @pl.when(ki == 0)
    def _():
        m_ref[...] = jnp.full(m_ref.shape, NEG_BIG, F32)
        acc_ref[...] = jnp.zeros(acc_ref.shape, F32)

    def step(masked):
        krot = krot_ref[...]
        if masked:
            k_chunk = lax.broadcasted_iota(jnp.int32, (tile, tile), 0) // CHUNK
            q_chunk = lax.broadcasted_iota(jnp.int32, (tile, tile), 1) // CHUNK
            visible = k_chunk <= q_chunk
        def scores(h):
            kh = jnp.concatenate([kn_ref[:, h * MLA_NOPE:(h + 1) * MLA_NOPE], krot], axis=1)
            return _dot_nt(kh, q_ref[:, h * MLA_QK_PAD:(h + 1) * MLA_QK_PAD])

        st_next = scores(0)
        for h in range(MLA_HEADS):
            st = st_next
            if h + 1 < MLA_HEADS:
                st_next = scores(h + 1)
            if masked:
                st = jnp.where(visible, st, NEG_BIG)
            m_prev = m_ref[h]
            m_new = jnp.maximum(m_prev, _col_max(st))
            alpha = jnp.exp2(m_prev - m_new)
            p = jnp.exp2((st - m_new).astype(BF16))
            m_ref[h] = m_new
            acc_ref[h] = alpha * acc_ref[h] + _dot(vt_ref[h * MLA_VT_ROWS:(h + 1) * MLA_VT_ROWS, :], p)

    @pl.when(ki != qi)
    def _():
        step(False)

    @pl.when(ki == qi)
    def _():
        step(True)
        for h in range(MLA_HEADS):
            acc = acc_ref[h]
            out_t = acc[:MLA_V, :] / acc[MLA_V:MLA_V + 1, :]
            o_ref[:, h * MLA_V:(h + 1) * MLA_V] = out_t.T.astype(o_ref.dtype)


def _attn_pairs(n_tiles):
    qi, ki = [], []
    for i in range(n_tiles):
        for j in range(i + 1):
            qi.append(i)
            ki.append(j)
    return np.asarray(qi, np.int32), np.asarray(ki, np.int32)


def _attention(q, kn, krot, vt, batch, seq, *, tile=512):
    qi, ki = _attn_pairs(seq // tile)
    n = seq // tile
    hw = MLA_HEADS * MLA_V
    grid_spec = pltpu.PrefetchScalarGridSpec(
        num_scalar_prefetch=2,
        grid=(batch, len(qi)),
        in_specs=[
            pl.BlockSpec((tile, MLA_HEADS * MLA_QK_PAD), lambda b, p, qi, ki: (b * n + qi[p], 0)),
            pl.BlockSpec((tile, hw), lambda b, p, qi, ki: (b * n + ki[p], 0)),
            pl.BlockSpec((tile, LANE), lambda b, p, qi, ki: (b * n + ki[p], 0)),
            pl.BlockSpec((MLA_HEADS * MLA_VT_ROWS, tile), lambda b, p, qi, ki: (0, b * n + ki[p])),
        ],
        out_specs=pl.BlockSpec((tile, hw), lambda b, p, qi, ki: (b * n + qi[p], 0)),
        scratch_shapes=[pltpu.VMEM((MLA_HEADS, 1, tile), F32),
                        pltpu.VMEM((MLA_HEADS, MLA_VT_ROWS, tile), F32)],
    )
    return pl.pallas_call(
        functools.partial(_attn_kernel, tile=tile),
        out_shape=jax.ShapeDtypeStruct((batch * seq, hw), BF16),
        grid_spec=grid_spec,
        compiler_params=_params(("parallel", "arbitrary")),
        name="mla_attention",
    )(jnp.asarray(qi), jnp.asarray(ki), q, kn, krot, vt)


def _gla_kernel(gq_ref, gk_ref, gv_ref, go_ref, glr_ref, whi_ref, wlo_ref, bg_ref, gn_ref,
                o_ref, st_ref, *, n_chunks):
    @pl.when(pl.program_id(1) == 0)
    def _():
        st_ref[...] = jnp.zeros(st_ref.shape, F32)

    row = lax.broadcasted_iota(jnp.int32, (CHUNK, CHUNK), 0)
    col = lax.broadcasted_iota(jnp.int32, (CHUNK, CHUNK), 1)
    tril = jnp.where(col <= row, 1.0, 0.0).astype(BF16)
    q_scale = GLA_DK ** -0.5

    def chunk_body(c, carry):
        rows = pl.ds(pl.multiple_of(c * CHUNK, CHUNK), CHUNK)
        glr = glr_ref[rows, :]
        logit = _dot(glr, whi_ref[...]) + _dot(glr, wlo_ref[...]) + bg_ref[...]
        lg = -(jnp.maximum(-logit, 0.0) + jnp.log1p(jnp.exp(-jnp.abs(logit)))) / GLA_GATE_NORM
        lg_hi, lg_lo = _split_hi_lo(lg)
        cum = _dot(tril, lg_hi) + _dot(tril, lg_lo)
        cum_end = cum[CHUNK - 1:CHUNK, :]
        k_dec = (gk_ref[rows, :].astype(F32) * jnp.exp(cum_end - cum)).astype(BF16)
        decay = jnp.exp(cum_end)
        qs = (gq_ref[rows, :].astype(F32) * q_scale).astype(BF16)
        for h in range(GLA_HEADS):
            ks = slice(h * GLA_DK, (h + 1) * GLA_DK)
            vs = slice(h * GLA_DV, (h + 1) * GLA_DV)
            upd_t = _dot_tn(gv_ref[rows, vs], k_dec[:, ks])
            st = st_ref[h] * decay[:, ks] + upd_t
            st_ref[h] = st
            o = _dot_nt(qs[:, ks], st.astype(BF16))
            o = _rms_norm(o, gn_ref[...])
            g = go_ref[rows, vs].astype(F32)
            o_ref[rows, vs] = (o * (g * _sigmoid(g))).astype(o_ref.dtype)
        return carry

    lax.fori_loop(0, n_chunks, chunk_body, 0)


def _gla(h, whi, wlo, bg, gn, batch, seq, *, tc=512):
    t = h.shape[0]
    nb = seq // tc
    kw, vw = GLA_HEADS * GLA_DK, GLA_HEADS * GLA_DV
    return pl.pallas_call(
        functools.partial(_gla_kernel, n_chunks=tc // CHUNK),
        out_shape=jax.ShapeDtypeStruct((t, vw), BF16),
        grid=(batch, nb),
        in_specs=[pl.BlockSpec((tc, kw), lambda b, i: (b * nb + i, COL_GQ // kw)),
                  pl.BlockSpec((tc, kw), lambda b, i: (b * nb + i, COL_GK // kw)),
                  pl.BlockSpec((tc, vw), lambda b, i: (b * nb + i, COL_GV // vw)),
                  pl.BlockSpec((tc, vw), lambda b, i: (b * nb + i, COL_GO // vw)),
                  pl.BlockSpec((tc, LANE), lambda b, i: (b * nb + i, COL_GLR // LANE)),
                  _resident(whi.shape), _resident(wlo.shape), _resident(bg.shape),
                  _resident(gn.shape)],
        out_specs=pl.BlockSpec((tc, vw), lambda b, i: (b * nb + i, 0)),
        scratch_shapes=[pltpu.VMEM((GLA_HEADS, GLA_DV, GLA_DK), F32)],
        compiler_params=_params(("parallel", "arbitrary")),
        name="gla",
    )(h, h, h, h, h, whi, wlo, bg, gn)


def _sgu_kernel(u_ref, v_ref, lg_ref, lb_ref, ws_ref, bs_ref, o_ref):
    tm = u_ref.shape[0]
    vn = _layer_norm(v_ref[...].astype(F32), lg_ref[...], lb_ref[...]).astype(BF16)
    pos_chunk_i = lax.broadcasted_iota(jnp.int32, (SGU_BLOCK, SGU_BLOCK), 0) // CHUNK
    pos_chunk_j = lax.broadcasted_iota(jnp.int32, (SGU_BLOCK, SGU_BLOCK), 1) // CHUNK
    causal = pos_chunk_j <= pos_chunk_i
    for g in range(SGU_GROUPS):
        cs = slice(g * LANE, (g + 1) * LANE)
        w = jnp.where(causal, ws_ref[g], 0.0).astype(BF16)
        for n in range(tm // SGU_BLOCK):
            rs = slice(n * SGU_BLOCK, (n + 1) * SGU_BLOCK)
            mixed = _dot(w, vn[rs, cs]) + bs_ref[:, cs]
            o_ref[rs, cs] = (u_ref[rs, cs].astype(F32) * mixed).astype(o_ref.dtype)


def _sgu(h, lg, lb, ws, bs_full, *, tm=256):
    t = h.shape[0]
    return pl.pallas_call(
        _sgu_kernel,
        out_shape=jax.ShapeDtypeStruct((t, SGU_HALF), BF16),
        grid=(t // tm,),
        in_specs=[pl.BlockSpec((tm, SGU_HALF), lambda i: (i, 0)),
                  pl.BlockSpec((tm, SGU_HALF), lambda i: (i, 1)),
                  _resident(lg.shape), _resident(lb.shape), _resident(ws.shape),
                  _resident(bs_full.shape)],
        out_specs=pl.BlockSpec((tm, SGU_HALF), lambda i: (i, 0)),
        compiler_params=_params(("parallel",)),
        name="sgu",
    )(h, h, lg, lb, ws, bs_full)


def _route(x1, rw2_ref, rwhi_ref, rb_ref):
    x_hi, x_lo = _split_hi_lo(x1)
    part = _dot_nt(rw2_ref[...], x_hi)
    logits = part[:N_EXPERTS] + part[N_EXPERTS:] + _dot_nt(rwhi_ref[...], x_lo)
    scores = _sigmoid(logits)
    biased = scores + rb_ref[...]
    b = [biased[e:e + 1, :] for e in range(N_EXPERTS)]
    s = [scores[e:e + 1, :] for e in range(N_EXPERTS)]

    def top2_sum(v):
        best = v[0] + v[1]
        for i in range(len(v)):
            for j in range(i + 1, len(v)):
                if (i, j) != (0, 1):
                    best = jnp.maximum(best, v[i] + v[j])
        return best

    def argmax_first(v):
        best_v, best_i = v[0], jnp.zeros(v[0].shape, jnp.int32)
        for i in range(1, len(v)):
            upd = v[i] > best_v
            best_i = jnp.where(upd, i, best_i)
            best_v = jnp.where(upd, v[i], best_v)
        return best_i

    def pick(v, idx):
        out = v[len(v) - 1]
        for i in range(len(v) - 2, -1, -1):
            out = jnp.where(idx == i, v[i], out)
        return out

    grp = [top2_sum(b[g * EXPERTS_PER_GROUP:(g + 1) * EXPERTS_PER_GROUP]) for g in range(N_GROUPS)]
    best_g = argmax_first(grp)
    cand_b = [pick([b[g * EXPERTS_PER_GROUP + j] for g in range(N_GROUPS)], best_g)
              for j in range(EXPERTS_PER_GROUP)]
    cand_s = [pick([s[g * EXPERTS_PER_GROUP + j] for g in range(N_GROUPS)], best_g)
              for j in range(EXPERTS_PER_GROUP)]
    i1 = argmax_first(cand_b)
    cand_b2 = [jnp.where(i1 == j, -jnp.inf, cand_b[j]) for j in range(EXPERTS_PER_GROUP)]
    i2 = argmax_first(cand_b2)
    g1 = pick(cand_s, i1)
    g2 = pick(cand_s, i2)
    den = g1 + g2
    w1, w2 = g1 / den, g2 / den
    lo, hi = jnp.minimum(i1, i2), jnp.maximum(i1, i2)
    pair = jnp.where(lo == 0, hi - 1, jnp.where(lo == 1, jnp.where(hi == 3, 3, 4), 5))
    slot_a = jnp.where(pair < 3, 0, jnp.where(pair < 5, 1, 3))
    first_is_a = i1 == slot_a
    cls = best_g * PAIRS_PER_GROUP + pair
    return cls, jnp.where(first_is_a, w1, w2), jnp.where(first_is_a, w2, w1)


def _outproj_kernel(*refs, n_in):
    a_refs = refs[:n_in]
    w_refs = refs[n_in:2 * n_in]
    (x_ref, g_ref, b_ref, rw2_ref, rwhi_ref, rb_ref,
     x1_ref, ri_ref, rg_ref, cnt_ref, carry_ref) = refs[2 * n_in:]
    tm = x_ref.shape[0]

    @pl.when(pl.program_id(0) == 0)
    def _():
        carry_ref[...] = jnp.zeros(carry_ref.shape, F32)

    mix = _dot(a_refs[0][...], w_refs[0][...])
    for a_ref, w_ref in zip(a_refs[1:], w_refs[1:]):
        mix = mix + _dot(a_ref[...], w_ref[...])
    x1 = _layer_norm(ALPHA * x_ref[...] + mix, g_ref[...], b_ref[...])
    x1_ref[...] = x1

    cls, gate_a, gate_b = _route(x1, rw2_ref, rwhi_ref, rb_ref)
    class_row = lax.broadcasted_iota(jnp.int32, (CLASS_ROWS, tm), 0)
    onehot = jnp.where(class_row == cls, 1.0, 0.0)
    t_prev = lax.broadcasted_iota(jnp.int32, (tm, tm), 0)
    t_cur = lax.broadcasted_iota(jnp.int32, (tm, tm), 1)
    before = jnp.where(t_prev < t_cur, 1.0, 0.0).astype(BF16)
    seen = _dot(onehot.astype(BF16), before) + carry_ref[...]
    rank = jnp.sum(onehot * seen, axis=0, keepdims=True)
    carry_ref[...] += jnp.sum(onehot, axis=1, keepdims=True)

    pad_i = jnp.zeros((ROUTE_ROWS - 2, tm), jnp.int32)
    ri_ref[...] = jnp.concatenate([cls, rank.astype(jnp.int32), pad_i], axis=0)
    rg_ref[...] = jnp.concatenate([gate_a, gate_b, pad_i.astype(F32)], axis=0)
    cnt_ref[...] = jnp.broadcast_to(carry_ref[...], cnt_ref.shape).astype(jnp.int32)


def _outproj(acts, ws, x, g, b, rw2, rwhi, rb, *, tm=256):
    t, d = acts[0].shape[0], x.shape[1]
    n_in = len(acts)
    in_specs = ([pl.BlockSpec((tm, a.shape[1]), lambda i: (i, 0)) for a in acts]
                + [_resident(w.shape) for w in ws]
                + [pl.BlockSpec((tm, d), lambda i: (i, 0)),
                   _resident(g.shape), _resident(b.shape), _resident(rw2.shape),
                   _resident(rwhi.shape), _resident(rb.shape)])
    return pl.pallas_call(
        functools.partial(_outproj_kernel, n_in=n_in),
        out_shape=(jax.ShapeDtypeStruct((t, d), F32),
                   jax.ShapeDtypeStruct((ROUTE_ROWS, t), jnp.int32),
                   jax.ShapeDtypeStruct((ROUTE_ROWS, t), F32),
                   jax.ShapeDtypeStruct((CLASS_ROWS, LANE), jnp.int32)),
        grid=(t // tm,),
        in_specs=in_specs,
        out_specs=(pl.BlockSpec((tm, d), lambda i: (i, 0)),
                   pl.BlockSpec((ROUTE_ROWS, tm), lambda i: (0, i)),
                   pl.BlockSpec((ROUTE_ROWS, tm), lambda i: (0, i)),
                   pl.BlockSpec((CLASS_ROWS, LANE), lambda i: (0, 0))),
        scratch_shapes=[pltpu.VMEM((CLASS_ROWS, 1), F32)],
        compiler_params=_params(("arbitrary",)),
        name="outproj_ln_route",
    )(*acts, *ws, x, g, b, rw2, rwhi, rb)


def _moe_plan(ri, rg, cnt, t, tm, n_tiles):
    cls, rank = ri[0], ri[1]
    counts = cnt[:N_CLASSES, 0]
    tiles_c = (counts + tm - 1) // tm
    tile_end = jnp.cumsum(tiles_c)
    tile_start = tile_end - tiles_c
    used = tile_end[-1]
    slot = tile_start[cls] * tm + rank
    per_token = jnp.stack([jnp.arange(1, t + 1, dtype=F32), rg[0], rg[1]], axis=1)
    per_slot = jnp.zeros((n_tiles * tm, 3), F32).at[slot].set(per_token, unique_indices=True)
    gates = per_slot[:, 1:]
    token = per_slot[:, 0].astype(jnp.int32) - 1
    occupied = token >= 0
    slot_id = jnp.arange(n_tiles * tm, dtype=jnp.int32)
    src_row = jnp.where(occupied, token, 0)
    dst_row = jnp.where(occupied, token, t + ((slot_id // tm) % 2) * tm + slot_id % tm)
    tile = jnp.arange(n_tiles, dtype=jnp.int32)
    covered = tile_end[None, :] <= jnp.minimum(tile, used - 1)[:, None]
    tile_cls = jnp.minimum(jnp.sum(covered.astype(jnp.int32), axis=1), N_CLASSES - 1)
    rows = jnp.clip(counts[tile_cls] - (tile - tile_start[tile_cls]) * tm, 0, tm)
    rows = jnp.where(tile < used, rows, 0).astype(jnp.int32)
    grp, pair = tile_cls // PAIRS_PER_GROUP, tile_cls % PAIRS_PER_GROUP
    expert_a = (grp * EXPERTS_PER_GROUP + jnp.asarray(PAIR_SLOT_A)[pair]).astype(jnp.int32)
    expert_b = (grp * EXPERTS_PER_GROUP + jnp.asarray(PAIR_SLOT_B)[pair]).astype(jnp.int32)
    return src_row, dst_row, gates, expert_a, expert_b, rows


def _moe_kernel(src_ref, dst_ref, ea_ref, eb_ref, rows_ref, x_hbm, gs_ref,
                wga_ref, wua_ref, wda_ref, wgb_ref, wub_ref, wdb_ref, g_ref, b_ref,
                o_hbm, xbuf, obuf, xb_ref, y_ref, gsem, ssem, *, tm, n_tiles, n_rows_out):
    i = pl.program_id(0)
    cur = i % 2
    other = 1 - cur

    def used_at(tile):
        inside = jnp.logical_and(tile >= 0, tile < n_tiles)
        return jnp.logical_and(inside, rows_ref[jnp.clip(tile, 0, n_tiles - 1)] > 0)

    used_prev2, used_prev, used_cur, used_next = (used_at(i - 2), used_at(i - 1), used_at(i),
                                                  used_at(i + 1))

    def gather_row(tile, r, buf, skew=0):
        row = src_ref[tile * tm + r + skew]
        pltpu.make_async_copy(x_hbm.at[pl.ds(row, 1), :],
                              xbuf.at[buf, pl.ds(r, 1), :], gsem.at[buf]).start()
        return row

    def scatter_row(tile, r, buf, skew=0):
        row = dst_ref[tile * tm + r + skew]
        pltpu.make_async_copy(obuf.at[buf, pl.ds(r, 1), :],
                              o_hbm.at[pl.ds(row, 1), :], ssem.at[buf]).start()
        return row

    def gather_loop(tile, buf):
        def body(r, carry):
            gather_row(tile, r, buf)
            return carry
        lax.fori_loop(0, tm, body, 0, unroll=8)

    def scatter_loop(tile, buf):
        def body(r, carry):
            scatter_row(tile, r, buf)
            return carry
        lax.fori_loop(0, tm, body, 0, unroll=8)

    def ffn():
        xb = xb_ref[...]
        gates = gs_ref[...]

        def expert(wg_ref, wu_ref, gate_col):
            gate = _dot(xb, wg_ref[0])
            up = _dot(xb, wu_ref[0])
            return ((gate * _sigmoid(gate)) * up * gate_col).astype(BF16)

        y_ref[...] = (_dot(expert(wga_ref, wua_ref, gates[:, 0:1]), wda_ref[0])
                      + _dot(expert(wgb_ref, wub_ref, gates[:, 1:2]), wdb_ref[0]))

    @pl.when(i == 0)
    def _():
        obuf[...] = jnp.zeros(obuf.shape, F32)
        for half in range(2):
            fill = pltpu.make_async_copy(obuf.at[half], o_hbm.at[pl.ds(n_rows_out + half * tm, tm), :],
                                         ssem.at[half])
            fill.start()
            fill.wait()

    @pl.when(jnp.logical_and(i == 0, used_cur))
    def _():
        gather_loop(0, 0)

    steady = jnp.logical_and(jnp.logical_and(used_prev, used_cur), used_next)

    @pl.when(steady)
    def _():
        for r in range(tm):
            gather_row(i + 1, r, other)

    @pl.when(used_cur)
    def _():
        pltpu.make_async_copy(x_hbm.at[pl.ds(0, tm), :], xbuf.at[cur], gsem.at[cur]).wait()
        xb_ref[...] = xbuf[cur].astype(BF16)

    @pl.when(steady)
    def _():
        for r in range(tm):
            scatter_row(i - 1, r, other)
        ffn()

    @pl.when(jnp.logical_not(steady))
    def _():
        @pl.when(used_next)
        def _():
            gather_loop(i + 1, other)

        @pl.when(used_prev)
        def _():
            scatter_loop(i - 1, other)

        @pl.when(used_cur)
        def _():
            ffn()

    @pl.when(used_prev2)
    def _():
        pltpu.make_async_copy(obuf.at[cur], o_hbm.at[pl.ds(0, tm), :], ssem.at[cur]).wait()

    @pl.when(used_cur)
    def _():
        obuf[cur] = _layer_norm(ALPHA * xbuf[cur] + y_ref[...], g_ref[...], b_ref[...])


def _moe(x, ri, rg, cnt, wg, wu, wd, g, b, *, tm=256):
    t, d = x.shape
    n_tiles = t // tm + N_CLASSES + 2
    src_row, dst_row, gates, expert_a, expert_b, rows = _moe_plan(ri, rg, cnt, t, tm, n_tiles)
    w_in = lambda sel: pl.BlockSpec((1, d, D_EXPERT), lambda i, s, dr, ea, eb, rows: (sel(ea, eb)[i], 0, 0))
    w_out = lambda sel: pl.BlockSpec((1, D_EXPERT, d), lambda i, s, dr, ea, eb, rows: (sel(ea, eb)[i], 0, 0))
    pick_a = lambda ea, eb: ea
    pick_b = lambda ea, eb: eb
    grid_spec = pltpu.PrefetchScalarGridSpec(
        num_scalar_prefetch=5,
        grid=(n_tiles,),
        in_specs=[pl.BlockSpec(memory_space=pl.ANY),
                  pl.BlockSpec((tm, 2), lambda i, *_: (i, 0)),
                  w_in(pick_a), w_in(pick_a), w_out(pick_a),
                  w_in(pick_b), w_in(pick_b), w_out(pick_b),
                  pl.BlockSpec(g.shape, lambda i, *_: (0, 0)),
                  pl.BlockSpec(b.shape, lambda i, *_: (0, 0))],
        out_specs=pl.BlockSpec(memory_space=pl.ANY),
        scratch_shapes=[pltpu.VMEM((2, tm, d), F32), pltpu.VMEM((2, tm, d), F32),
                        pltpu.VMEM((tm, d), BF16), pltpu.VMEM((tm, d), F32),
                        pltpu.SemaphoreType.DMA((2,)), pltpu.SemaphoreType.DMA((2,))],
    )
    return pl.pallas_call(
        functools.partial(_moe_kernel, tm=tm, n_tiles=n_tiles, n_rows_out=t),
        out_shape=jax.ShapeDtypeStruct((t + 2 * tm, d), F32),
        grid_spec=grid_spec,
        compiler_params=_params(("arbitrary",)),
        name="moe_ln",
    )(src_row, dst_row, expert_a, expert_b, rows, x, gates, wg, wu, wd, wg, wu, wd, g, b)


def _rotate_half_cols(w):
    half = MLA_ROPE // 2
    return jnp.concatenate([-w[..., half:], w[..., :half]], axis=-1)


def _pack_even_in(w_in):
    cq, ckv, kr, gq, gk, gv, glr, go = jnp.split(
        w_in, np.cumsum([512, 512, 64, 512, 512, 1024, 16])[:].tolist(), axis=1)
    pad = jnp.zeros((w_in.shape[0], EVEN_IN_PAD - COL_GLR - GLA_GATE_RANK), w_in.dtype)
    return jnp.concatenate([cq, ckv, gq, gk, gv, go, kr, _rotate_half_cols(kr), glr, pad],
                           axis=1).astype(BF16)


def _pack_wq(w_uq):
    r = w_uq.shape[0]
    nope, rope = w_uq[..., :MLA_NOPE], w_uq[..., MLA_NOPE:]
    zeros = jnp.zeros((r, MLA_HEADS, MLA_QK_PAD - MLA_NOPE - MLA_ROPE), w_uq.dtype)
    wqa = jnp.concatenate([nope, rope, zeros], axis=-1).reshape(r, MLA_HEADS * MLA_QK_PAD)
    wqb = jnp.concatenate([_rotate_half_cols(rope), zeros], axis=-1).reshape(r, MLA_HEADS * LANE)
    return wqa.astype(BF16), wqb.astype(BF16)


def _pack_wkv(w_ukv):
    r = w_ukv.shape[0]
    kn = w_ukv[..., :MLA_NOPE].reshape(r, MLA_HEADS * MLA_NOPE)
    v = w_ukv[..., MLA_NOPE:].reshape(r, MLA_HEADS * MLA_V)
    return kn.astype(BF16), v.T.astype(BF16)


def kernel(x, positions, ev_w_in, ev_q_norm, ev_kv_norm, ev_w_uq, ev_w_ukv, ev_w_gk2, ev_b_gk2, ev_gla_norm, ev_w_out, od_w_in, od_sgu_ln_g, od_sgu_ln_b, od_w_s, od_b_s, od_w_out, ln_mix_g, ln_mix_b, ln_ffn_g, ln_ffn_b, router_w, router_b, moe_w_gate, moe_w_up, moe_w_down):
    batch, seq, d = x.shape
    t = batch * seq
    xf = x.reshape(t, d)
    pos = positions.reshape(t, 1)

    inv = 1.0 / (ROPE_THETA ** (jnp.arange(0, MLA_ROPE, 2, dtype=F32) / MLA_ROPE))
    inv_row = jnp.concatenate([inv, inv, jnp.zeros((LANE - MLA_ROPE,), F32)]).reshape(1, LANE)

    rw_t = router_w.T
    rw_hi = rw_t.astype(BF16)
    rw_lo = (rw_t - rw_hi.astype(F32)).astype(BF16)
    rw2 = jnp.concatenate([rw_hi, rw_lo], axis=0)
    rb = router_b.astype(F32).reshape(N_EXPERTS, 1)

    for i in range(DEPTH):
        j = i // 2
        if i % 2 == 0:
            h = _inproj(xf, _pack_even_in(ev_w_in[j]), t, gelu=False)
            wqa, wqb = _pack_wq(ev_w_uq[j])
            wkn, wvt = _pack_wkv(ev_w_ukv[j])
            q, kn, krot, vt = _mla_proj(h, pos, inv_row, ev_q_norm[j].reshape(1, -1),
                                        ev_kv_norm[j].reshape(1, -1), wqa, wqb, wkn, wvt)
            o_a = _attention(q, kn, krot, vt, batch, seq)
            wg2 = jnp.concatenate(
                [ev_w_gk2[j], jnp.zeros((LANE - GLA_GATE_RANK, GLA_HEADS * GLA_DK), F32)], axis=0)
            wg2_hi = wg2.astype(BF16)
            wg2_lo = (wg2 - wg2_hi.astype(F32)).astype(BF16)
            o_b = _gla(h, wg2_hi, wg2_lo, ev_b_gk2[j].reshape(1, -1),
                       ev_gla_norm[j].reshape(1, -1), batch, seq)
            w_out = ev_w_out[j].astype(BF16)
            split = MLA_HEADS * MLA_V
            acts, ws = [o_a, o_b], [w_out[:split], w_out[split:]]
        else:
            h = _inproj(xf, od_w_in[j].astype(BF16), t, gelu=True)
            bs_full = jnp.repeat(od_b_s[j].T, LANE, axis=1)
            gated = _sgu(h, od_sgu_ln_g[j].reshape(1, -1), od_sgu_ln_b[j].reshape(1, -1),
                         od_w_s[j], bs_full)
            acts, ws = [gated], [od_w_out[j].astype(BF16)]
        x1, ri, rg, cnt = _outproj(acts, ws, xf, ln_mix_g[i].reshape(1, -1),
                                   ln_mix_b[i].reshape(1, -1), rw2, rw_hi, rb)
        xf = _moe(x1, ri, rg, cnt,
                  moe_w_gate[i].astype(BF16), moe_w_up[i].astype(BF16),
                  moe_w_down[i].astype(BF16),
                  ln_ffn_g[i].reshape(1, -1), ln_ffn_b[i].reshape(1, -1))
    return xf[:t].reshape(batch, seq, d)
```

```python
import functools

import jax
import jax.numpy as jnp
import numpy as np
from jax import lax
from jax.experimental import pallas as pl
from jax.experimental.pallas import tpu as pltpu

F32 = jnp.float32
BF16 = jnp.bfloat16

D_MODEL = 2048
DEPTH = 2
CHUNK = 64
MLA_HEADS = 8
MLA_RANK = 512
MLA_NOPE = 128
MLA_ROPE = 64
MLA_V = 128
MLA_QK_PAD = 256
MLA_VT_ROWS = 144
LOG2_E = 1.4426950408889634
ROPE_THETA = 10000.0
GLA_HEADS = 4
GLA_DK = 128
GLA_DV = 256
GLA_GATE_RANK = 16
GLA_GATE_NORM = 16.0
SGU_BLOCK = 128
SGU_HALF = D_MODEL
SGU_GROUPS = 16
N_EXPERTS = 16
N_GROUPS = 4
EXPERTS_PER_GROUP = 4
D_EXPERT = 512
PAIR_SLOT_A = np.array([0, 0, 0, 1, 1, 3], np.int32)
PAIR_SLOT_B = np.array([1, 2, 3, 3, 2, 2], np.int32)
PAIRS_PER_GROUP = 6
N_CLASSES = N_GROUPS * PAIRS_PER_GROUP
ROW_BITS = 24
CLASS_ROWS = 32
ROUTE_ROWS = 8
ALPHA = (2.0 * DEPTH) ** 0.25
EPS = 1e-6
NEG_BIG = -1e30

LANE = 128
VMEM_LIMIT = 56 * 1024 * 1024

COL_CQ, COL_CKV, COL_GQ, COL_GK = 0, 512, 1024, 1536
COL_GV, COL_GO, COL_KR, COL_GLR = 2048, 3072, 4096, 4224
EVEN_IN_PAD = 4352


def _dot(a, b):
    return jnp.dot(a, b, preferred_element_type=F32)


def _dot_nt(a, b):
    return lax.dot_general(a, b, (((1,), (1,)), ((), ())), preferred_element_type=F32)


def _dot_tn(a, b):
    return lax.dot_general(a, b, (((0,), (0,)), ((), ())), preferred_element_type=F32)


def _split_hi_lo(x):
    hi = x.astype(BF16)
    lo = (x - hi.astype(F32)).astype(BF16)
    return hi, lo


def _layer_norm(y, g, b):
    mu = jnp.mean(y, axis=-1, keepdims=True)
    d = y - mu
    var = jnp.mean(d * d, axis=-1, keepdims=True)
    return d * lax.rsqrt(var + EPS) * g + b


def _rms_norm(y, g):
    return y * lax.rsqrt(jnp.mean(y * y, axis=-1, keepdims=True) + EPS) * g


def _sigmoid(x):
    return 1.0 / (1.0 + jnp.exp(-x))


def _params(sem):
    return pltpu.CompilerParams(dimension_semantics=sem, vmem_limit_bytes=VMEM_LIMIT)


def _resident(shape):
    nd = len(shape)
    return pl.BlockSpec(shape, lambda *_: (0,) * nd, pipeline_mode=pl.Buffered(1))


def _inproj_kernel(x_ref, w_ref, o_ref, *, gelu, n_chunk):
    xb = x_ref[...].astype(BF16)
    n = w_ref.shape[1]
    for n0 in range(0, n, n_chunk):
        nw = min(n_chunk, n - n0)
        y = _dot(xb, w_ref[:, n0:n0 + nw])
        if gelu:
            y = 0.5 * y * (1.0 + jnp.tanh(0.7978845608028654 * (y + 0.044715 * (y * y * y))))
        o_ref[:, n0:n0 + nw] = y.astype(o_ref.dtype)


def _inproj(x, w, t, *, gelu, tm=512):
    d = x.shape[1]
    n = w.shape[1]
    return pl.pallas_call(
        functools.partial(_inproj_kernel, gelu=gelu, n_chunk=512),
        out_shape=jax.ShapeDtypeStruct((t, n), BF16),
        grid=(t // tm,),
        in_specs=[pl.BlockSpec((tm, d), lambda i: (i, 0)), _resident((d, n))],
        out_specs=pl.BlockSpec((tm, n), lambda i: (i, 0)),
        compiler_params=_params(("parallel",)),
        name="inproj_gelu" if gelu else "inproj",
    )(x, w)


def _mla_proj_kernel(cq_ref, ckv_ref, kr_ref, pos_ref, inv_ref, gq_ref, gkv_ref,
                     wqa_ref, wqb_ref, wkn_ref, wvt_ref, q_ref, kn_ref, krot_ref, vt_ref):
    scale = (MLA_NOPE + MLA_ROPE) ** -0.5 * LOG2_E
    ang = pos_ref[...].astype(F32) * inv_ref[...]
    lane = lax.broadcasted_iota(jnp.int32, ang.shape, 1)
    rot_lane = lane < MLA_ROPE
    cos = jnp.where(rot_lane, jnp.cos(ang), 0.0)
    sin = jnp.where(rot_lane, jnp.sin(ang), 0.0)

    nq = _rms_norm(cq_ref[...].astype(F32), gq_ref[...]).astype(BF16)
    for h in range(MLA_HEADS):
        a = _dot(nq, wqa_ref[:, h * MLA_QK_PAD:(h + 1) * MLA_QK_PAD])
        b = _dot(nq, wqb_ref[:, h * LANE:(h + 1) * LANE])
        q_ref[:, h * MLA_QK_PAD:h * MLA_QK_PAD + LANE] = (a[:, :LANE] * scale).astype(BF16)
        q_ref[:, h * MLA_QK_PAD + LANE:(h + 1) * MLA_QK_PAD] = (
            (a[:, LANE:] * cos + b * sin) * scale).astype(BF16)

    nkv = _rms_norm(ckv_ref[...].astype(F32), gkv_ref[...]).astype(BF16)
    kn_ref[...] = _dot(nkv, wkn_ref[...]).astype(BF16)
    tm = nkv.shape[0]
    tail_row = lax.broadcasted_iota(jnp.int32, (MLA_VT_ROWS - MLA_V, tm), 0)
    tail = jnp.where(tail_row == 0, 1.0, 0.0).astype(BF16)
    for h in range(MLA_HEADS):
        vt = _dot_nt(wvt_ref[h * MLA_V:(h + 1) * MLA_V, :], nkv)
        vt_ref[h * MLA_VT_ROWS:h * MLA_VT_ROWS + MLA_V, :] = vt.astype(BF16)
        vt_ref[h * MLA_VT_ROWS + MLA_V:(h + 1) * MLA_VT_ROWS, :] = tail

    kr = kr_ref[...].astype(F32)
    krot_ref[...] = (kr * cos + pltpu.roll(kr, MLA_ROPE, axis=1) * sin).astype(BF16)


def _mla_proj(h, pos, inv, gq, gkv, wqa, wqb, wkn, wvt, *, tm=512):
    t = h.shape[0]
    hw = MLA_HEADS * MLA_NOPE
    return pl.pallas_call(
        _mla_proj_kernel,
        out_shape=(jax.ShapeDtypeStruct((t, MLA_HEADS * MLA_QK_PAD), BF16),
                   jax.ShapeDtypeStruct((t, hw), BF16),
                   jax.ShapeDtypeStruct((t, LANE), BF16),
                   jax.ShapeDtypeStruct((MLA_HEADS * MLA_VT_ROWS, t), BF16)),
        grid=(t // tm,),
        in_specs=[pl.BlockSpec((tm, MLA_RANK), lambda i: (i, COL_CQ // MLA_RANK)),
                  pl.BlockSpec((tm, MLA_RANK), lambda i: (i, COL_CKV // MLA_RANK)),
                  pl.BlockSpec((tm, LANE), lambda i: (i, COL_KR // LANE)),
                  pl.BlockSpec((tm, 1), lambda i: (i, 0)),
                  _resident((1, LANE)), _resident((1, MLA_RANK)), _resident((1, MLA_RANK)),
                  _resident(wqa.shape), _resident(wqb.shape), _resident(wkn.shape),
                  _resident(wvt.shape)],
        out_specs=(pl.BlockSpec((tm, MLA_HEADS * MLA_QK_PAD), lambda i: (i, 0)),
                   pl.BlockSpec((tm, hw), lambda i: (i, 0)),
                   pl.BlockSpec((tm, LANE), lambda i: (i, 0)),
                   pl.BlockSpec((MLA_HEADS * MLA_VT_ROWS, tm), lambda i: (0, i))),
        compiler_params=_params(("parallel",)),
        name="mla_proj",
    )(h, h, h, pos, inv, gq, gkv, wqa, wqb, wkn, wvt)


def _col_max(x):
    rows, cols = x.shape
    part = jnp.max(x.reshape(rows // 64, 64, cols), axis=0)
    return jnp.max(part, axis=0, keepdims=True)


def _attn_kernel(qi_ref, ki_ref, q_ref, kn_ref, krot_ref, vt_ref, o_ref, m_ref, acc_ref, *, tile):
    p_id = pl.program_id(1)
    qi = qi_ref[p_id]
    ki = ki_ref[p_id]

    @pl.when(ki == 0)
    def _():
        m_ref[...] = jnp.full(m_ref.shape, NEG_BIG, F32)
        acc_ref[...] = jnp.zeros(acc_ref.shape, F32)

    def step(masked):
        krot = krot_ref[...]
        if masked:
            k_chunk = lax.broadcasted_iota(jnp.int32, (tile, tile), 0) // CHUNK
            q_chunk = lax.broadcasted_iota(jnp.int32, (tile, tile), 1) // CHUNK
            visible = k_chunk <= q_chunk
        def scores(h):
            kh = jnp.concatenate([kn_ref[:, h * MLA_NOPE:(h + 1) * MLA_NOPE], krot], axis=1)
            return _dot_nt(kh, q_ref[:, h * MLA_QK_PAD:(h + 1) * MLA_QK_PAD])

        st_next = scores(0)
        for h in range(MLA_HEADS):
            st = st_next
            if h + 1 < MLA_HEADS:
                st_next = scores(h + 1)
            if masked:
                st = jnp.where(visible, st, NEG_BIG)
            m_prev = m_ref[h]
            m_new = jnp.maximum(m_prev, _col_max(st))
            alpha = jnp.exp2(m_prev - m_new)
            p = jnp.exp2((st - m_new).astype(BF16))
            m_ref[h] = m_new
            acc_ref[h] = alpha * acc_ref[h] + _dot(vt_ref[h * MLA_VT_ROWS:(h + 1) * MLA_VT_ROWS, :], p)

    @pl.when(ki != qi)
    def _():
        step(False)

    @pl.when(ki == qi)
    def _():
        step(True)
        for h in range(MLA_HEADS):
            acc = acc_ref[h]
            out_t = acc[:MLA_V, :] / acc[MLA_V:MLA_V + 1, :]
            o_ref[:, h * MLA_V:(h + 1) * MLA_V] = out_t.T.astype(o_ref.dtype)


def _attn_pairs(n_tiles):
    qi, ki = [], []
    for i in range(n_tiles):
        for j in range(i + 1):
            qi.append(i)
            ki.append(j)
    return np.asarray(qi, np.int32), np.asarray(ki, np.int32)


def _attention(q, kn, krot, vt, batch, seq, *, tile=512):
    qi, ki = _attn_pairs(seq // tile)
    n = seq // tile
    hw = MLA_HEADS * MLA_V
    grid_spec = pltpu.PrefetchScalarGridSpec(
        num_scalar_prefetch=2,
        grid=(batch, len(qi)),
        in_specs=[
            pl.BlockSpec((tile, MLA_HEADS * MLA_QK_PAD), lambda b, p, qi, ki: (b * n + qi[p], 0)),
            pl.BlockSpec((tile, hw), lambda b, p, qi, ki: (b * n + ki[p], 0)),
            pl.BlockSpec((tile, LANE), lambda b, p, qi, ki: (b * n + ki[p], 0)),
            pl.BlockSpec((MLA_HEADS * MLA_VT_ROWS, tile), lambda b, p, qi, ki: (0, b * n + ki[p])),
        ],
        out_specs=pl.BlockSpec((tile, hw), lambda b, p, qi, ki: (b * n + qi[p], 0)),
        scratch_shapes=[pltpu.VMEM((MLA_HEADS, 1, tile), F32),
                        pltpu.VMEM((MLA_HEADS, MLA_VT_ROWS, tile), F32)],
    )
    return pl.pallas_call(
        functools.partial(_attn_kernel, tile=tile),
        out_shape=jax.ShapeDtypeStruct((batch * seq, hw), BF16),
        grid_spec=grid_spec,
        compiler_params=_params(("parallel", "arbitrary")),
        name="mla_attention",
    )(jnp.asarray(qi), jnp.asarray(ki), q, kn, krot, vt)


def _gla_kernel(gq_ref, gk_ref, gv_ref, go_ref, glr_ref, whi_ref, wlo_ref, bg_ref, gn_ref,
                o_ref, st_ref, *, n_chunks):
    @pl.when(pl.program_id(1) == 0)
    def _():
        st_ref[...] = jnp.zeros(st_ref.shape, F32)

    row = lax.broadcasted_iota(jnp.int32, (CHUNK, CHUNK), 0)
    col = lax.broadcasted_iota(jnp.int32, (CHUNK, CHUNK), 1)
    tril = jnp.where(col <= row, 1.0, 0.0).astype(BF16)
    q_scale = GLA_DK ** -0.5

    def chunk_body(c, carry):
        rows = pl.ds(pl.multiple_of(c * CHUNK, CHUNK), CHUNK)
        glr = glr_ref[rows, :]
        logit = _dot(glr, whi_ref[...]) + _dot(glr, wlo_ref[...]) + bg_ref[...]
        lg = -(jnp.maximum(-logit, 0.0) + jnp.log1p(jnp.exp(-jnp.abs(logit)))) / GLA_GATE_NORM
        lg_hi, lg_lo = _split_hi_lo(lg)
        cum = _dot(tril, lg_hi) + _dot(tril, lg_lo)
        cum_end = cum[CHUNK - 1:CHUNK, :]
        k_dec = (gk_ref[rows, :].astype(F32) * jnp.exp(cum_end - cum)).astype(BF16)
        decay = jnp.exp(cum_end)
        qs = (gq_ref[rows, :].astype(F32) * q_scale).astype(BF16)
        for h in range(GLA_HEADS):
            ks = slice(h * GLA_DK, (h + 1) * GLA_DK)
            vs = slice(h * GLA_DV, (h + 1) * GLA_DV)
            upd_t = _dot_tn(gv_ref[rows, vs], k_dec[:, ks])
            st = st_ref[h] * decay[:, ks] + upd_t
            st_ref[h] = st
            o = _dot_nt(qs[:, ks], st.astype(BF16))
            o = _rms_norm(o, gn_ref[...])
            g = go_ref[rows, vs].astype(F32)
            o_ref[rows, vs] = (o * (g * _sigmoid(g))).astype(o_ref.dtype)
        return carry

    lax.fori_loop(0, n_chunks, chunk_body, 0)


def _gla(h, whi, wlo, bg, gn, batch, seq, *, tc=512):
    t = h.shape[0]
    nb = seq // tc
    kw, vw = GLA_HEADS * GLA_DK, GLA_HEADS * GLA_DV
    return pl.pallas_call(
        functools.partial(_gla_kernel, n_chunks=tc // CHUNK),
        out_shape=jax.ShapeDtypeStruct((t, vw), BF16),
        grid=(batch, nb),
        in_specs=[pl.BlockSpec((tc, kw), lambda b, i: (b * nb + i, COL_GQ // kw)),
                  pl.BlockSpec((tc, kw), lambda b, i: (b * nb + i, COL_GK // kw)),
                  pl.BlockSpec((tc, vw), lambda b, i: (b * nb + i, COL_GV // vw)),
                  pl.BlockSpec((tc, vw), lambda b, i: (b * nb + i, COL_GO // vw)),
                  pl.BlockSpec((tc, LANE), lambda b, i: (b * nb + i, COL_GLR // LANE)),
                  _resident(whi.shape), _resident(wlo.shape), _resident(bg.shape),
                  _resident(gn.shape)],
        out_specs=pl.BlockSpec((tc, vw), lambda b, i: (b * nb + i, 0)),
        scratch_shapes=[pltpu.VMEM((GLA_HEADS, GLA_DV, GLA_DK), F32)],
        compiler_params=_params(("parallel", "arbitrary")),
        name="gla",
    )(h, h, h, h, h, whi, wlo, bg, gn)


def _sgu_kernel(u_ref, v_ref, lg_ref, lb_ref, ws_ref, bs_ref, o_ref):
    tm = u_ref.shape[0]
    vn = _layer_norm(v_ref[...].astype(F32), lg_ref[...], lb_ref[...]).astype(BF16)
    pos_chunk_i = lax.broadcasted_iota(jnp.int32, (SGU_BLOCK, SGU_BLOCK), 0) // CHUNK
    pos_chunk_j = lax.broadcasted_iota(jnp.int32, (SGU_BLOCK, SGU_BLOCK), 1) // CHUNK
    causal = pos_chunk_j <= pos_chunk_i
    for g in range(SGU_GROUPS):
        cs = slice(g * LANE, (g + 1) * LANE)
        w = jnp.where(causal, ws_ref[g], 0.0).astype(BF16)
        for n in range(tm // SGU_BLOCK):
            rs = slice(n * SGU_BLOCK, (n + 1) * SGU_BLOCK)
            mixed = _dot(w, vn[rs, cs]) + bs_ref[:, cs]
            o_ref[rs, cs] = (u_ref[rs, cs].astype(F32) * mixed).astype(o_ref.dtype)


def _sgu(h, lg, lb, ws, bs_full, *, tm=256):
    t = h.shape[0]
    return pl.pallas_call(
        _sgu_kernel,
        out_shape=jax.ShapeDtypeStruct((t, SGU_HALF), BF16),
        grid=(t // tm,),
        in_specs=[pl.BlockSpec((tm, SGU_HALF), lambda i: (i, 0)),
                  pl.BlockSpec((tm, SGU_HALF), lambda i: (i, 1)),
                  _resident(lg.shape), _resident(lb.shape), _resident(ws.shape),
                  _resident(bs_full.shape)],
        out_specs=pl.BlockSpec((tm, SGU_HALF), lambda i: (i, 0)),
        compiler_params=_params(("parallel",)),
        name="sgu",
    )(h, h, lg, lb, ws, bs_full)


def _route_logits(x1, rw2_ref, rwhi_ref):
    x_hi, x_lo = _split_hi_lo(x1)
    part = _dot_nt(rw2_ref[...], x_hi)
    return part[:N_EXPERTS] + part[N_EXPERTS:] + _dot_nt(rwhi_ref[...], x_lo)


def _route_select(logits, rb_ref):
    scores = _sigmoid(logits)
    biased = scores + rb_ref[...]
    b = [biased[e:e + 1, :] for e in range(N_EXPERTS)]
    s = [scores[e:e + 1, :] for e in range(N_EXPERTS)]

    def top2_sum(v):
        best = v[0] + v[1]
        for i in range(len(v)):
            for j in range(i + 1, len(v)):
                if (i, j) != (0, 1):
                    best = jnp.maximum(best, v[i] + v[j])
        return best

    def argmax_first(v):
        best_v, best_i = v[0], jnp.zeros(v[0].shape, jnp.int32)
        for i in range(1, len(v)):
            upd = v[i] > best_v
            best_i = jnp.where(upd, i, best_i)
            best_v = jnp.where(upd, v[i], best_v)
        return best_i

    def pick(v, idx):
        out = v[len(v) - 1]
        for i in range(len(v) - 2, -1, -1):
            out = jnp.where(idx == i, v[i], out)
        return out

    grp = [top2_sum(b[g * EXPERTS_PER_GROUP:(g + 1) * EXPERTS_PER_GROUP]) for g in range(N_GROUPS)]
    best_g = argmax_first(grp)
    cand_b = [pick([b[g * EXPERTS_PER_GROUP + j] for g in range(N_GROUPS)], best_g)
              for j in range(EXPERTS_PER_GROUP)]
    cand_s = [pick([s[g * EXPERTS_PER_GROUP + j] for g in range(N_GROUPS)], best_g)
              for j in range(EXPERTS_PER_GROUP)]
    i1 = argmax_first(cand_b)
    cand_b2 = [jnp.where(i1 == j, -jnp.inf, cand_b[j]) for j in range(EXPERTS_PER_GROUP)]
    i2 = argmax_first(cand_b2)
    g1 = pick(cand_s, i1)
    g2 = pick(cand_s, i2)
    den = g1 + g2
    w1, w2 = g1 / den, g2 / den
    lo, hi = jnp.minimum(i1, i2), jnp.maximum(i1, i2)
    pair = jnp.where(lo == 0, hi - 1, jnp.where(lo == 1, jnp.where(hi == 3, 3, 4), 5))
    slot_a = jnp.where(pair < 3, 0, jnp.where(pair < 5, 1, 3))
    first_is_a = i1 == slot_a
    cls = best_g * PAIRS_PER_GROUP + pair
    return cls, jnp.where(first_is_a, w1, w2), jnp.where(first_is_a, w2, w1)


def _outproj_kernel(*refs, n_in):
    a_refs = refs[:n_in]
    w_refs = refs[n_in:2 * n_in]
    (x_ref, g_ref, b_ref, rw2_ref, rwhi_ref, rb_ref,
     x1_ref, ri_ref, rg_ref, cnt_ref, carry_ref, prev_ref) = refs[2 * n_in:]
    tm = x_ref.shape[0]
    i = pl.program_id(0)

    @pl.when(i == 0)
    def _():
        carry_ref[...] = jnp.zeros(carry_ref.shape, F32)
        prev_ref[...] = jnp.zeros(prev_ref.shape, F32)

    counted = jnp.where(i > 0, 1.0, 0.0)
    d_out = x_ref.shape[1]
    n_chunk = d_out // 4

    def project(c):
        cols = slice(c * n_chunk, (c + 1) * n_chunk)
        acc = _dot(a_refs[0][...], w_refs[0][:, cols])
        for a_ref, w_ref in zip(a_refs[1:], w_refs[1:]):
            acc = acc + _dot(a_ref[...], w_ref[:, cols])
        return acc

    mix = [project(0)]
    logits = _route_logits(prev_ref[...], rw2_ref, rwhi_ref)
    mix.append(project(1))
    cls, gate_a, gate_b = _route_select(logits, rb_ref)
    class_row = lax.broadcasted_iota(jnp.int32, (CLASS_ROWS, tm), 0)
    onehot = jnp.where(class_row == cls, 1.0, 0.0)
    t_prev = lax.broadcasted_iota(jnp.int32, (tm, tm), 0)
    t_cur = lax.broadcasted_iota(jnp.int32, (tm, tm), 1)
    before = jnp.where(t_prev < t_cur, 1.0, 0.0).astype(BF16)
    mix.append(project(2))
    seen = _dot(onehot.astype(BF16), before) + carry_ref[...]
    mix.append(project(3))
    rank = jnp.sum(onehot * seen, axis=0, keepdims=True)
    carry_ref[...] += counted * jnp.sum(onehot, axis=1, keepdims=True)

    pad_i = jnp.zeros((ROUTE_ROWS - 2, tm), jnp.int32)
    ri_ref[...] = jnp.concatenate([cls, rank.astype(jnp.int32), pad_i], axis=0)
    rg_ref[...] = jnp.concatenate([gate_a, gate_b, pad_i.astype(F32)], axis=0)
    cnt_ref[...] = jnp.broadcast_to(carry_ref[...], cnt_ref.shape).astype(jnp.int32)

    x1 = _layer_norm(ALPHA * x_ref[...] + jnp.concatenate(mix, axis=1), g_ref[...], b_ref[...])
    x1_ref[...] = x1
    prev_ref[...] = x1


def _outproj(acts, ws, x, g, b, rw2, rwhi, rb, *, tm=256):
    t, d = acts[0].shape[0], x.shape[1]
    n_in = len(acts)
    n = t // tm
    cur = lambda i: (jnp.minimum(i, n - 1), 0)
    routed = lambda i: (0, jnp.maximum(i - 1, 0))
    in_specs = ([pl.BlockSpec((tm, a.shape[1]), cur) for a in acts]
                + [_resident(w.shape) for w in ws]
                + [pl.BlockSpec((tm, d), cur),
                   _resident(g.shape), _resident(b.shape), _resident(rw2.shape),
                   _resident(rwhi.shape), _resident(rb.shape)])
    return pl.pallas_call(
        functools.partial(_outproj_kernel, n_in=n_in),
        out_shape=(jax.ShapeDtypeStruct((t, d), F32),
                   jax.ShapeDtypeStruct((ROUTE_ROWS, t), jnp.int32),
                   jax.ShapeDtypeStruct((ROUTE_ROWS, t), F32),
                   jax.ShapeDtypeStruct((CLASS_ROWS, LANE), jnp.int32)),
        grid=(n + 1,),
        in_specs=in_specs,
        out_specs=(pl.BlockSpec((tm, d), cur),
                   pl.BlockSpec((ROUTE_ROWS, tm), routed),
                   pl.BlockSpec((ROUTE_ROWS, tm), routed),
                   pl.BlockSpec((CLASS_ROWS, LANE), lambda i: (0, 0))),
        scratch_shapes=[pltpu.VMEM((CLASS_ROWS, 1), F32), pltpu.VMEM((tm, d), F32)],
        compiler_params=_params(("arbitrary",)),
        name="outproj_ln_route",
    )(*acts, *ws, x, g, b, rw2, rwhi, rb)


def _moe_plan(ri, rg, cnt, t, tm, n_tiles):
    cls, rank = ri[0], ri[1]
    counts = cnt[:N_CLASSES, 0]
    tiles_c = (counts + tm - 1) // tm
    tile_end = jnp.cumsum(tiles_c)
    tile_start = tile_end - tiles_c
    used = tile_end[-1]
    slot = tile_start[cls] * tm + rank
    per_token = jnp.stack([jnp.arange(1, t + 1, dtype=F32), rg[0], rg[1]], axis=1)
    per_slot = jnp.zeros((n_tiles * tm, 3), F32).at[slot].set(per_token, unique_indices=True)
    gates = per_slot[:, 1:]
    token = per_slot[:, 0].astype(jnp.int32) - 1
    occupied = token >= 0
    slot_id = jnp.arange(n_tiles * tm, dtype=jnp.int32)
    src_row = jnp.where(occupied, token, 0)
    dst_row = jnp.where(occupied, token, t + ((slot_id // tm) % 2) * tm + slot_id % tm)
    tile = jnp.arange(n_tiles, dtype=jnp.int32)
    covered = tile_end[None, :] <= jnp.minimum(tile, used - 1)[:, None]
    tile_cls = jnp.minimum(jnp.sum(covered.astype(jnp.int32), axis=1), N_CLASSES - 1)
    rows = jnp.clip(counts[tile_cls] - (tile - tile_start[tile_cls]) * tm, 0, tm)
    rows = jnp.where(tile < used, rows, 0).astype(jnp.int32)
    grp, pair = tile_cls // PAIRS_PER_GROUP, tile_cls % PAIRS_PER_GROUP
    expert_a = (grp * EXPERTS_PER_GROUP + jnp.asarray(PAIR_SLOT_A)[pair]).astype(jnp.int32)
    expert_b = (grp * EXPERTS_PER_GROUP + jnp.asarray(PAIR_SLOT_B)[pair]).astype(jnp.int32)
    return src_row, dst_row, gates, expert_a, expert_b, rows


def _moe_kernel(src_ref, dst_ref, ea_ref, eb_ref, rows_ref, x_hbm, gs_ref,
                wga_ref, wua_ref, wda_ref, wgb_ref, wub_ref, wdb_ref, g_ref, b_ref,
                o_hbm, xbuf, obuf, xb_ref, y_ref, gsem, ssem, *, tm, n_tiles, n_rows_out):
    i = pl.program_id(0)
    cur = i % 2
    other = 1 - cur

    def used_at(tile):
        inside = jnp.logical_and(tile >= 0, tile < n_tiles)
        return jnp.logical_and(inside, rows_ref[jnp.clip(tile, 0, n_tiles - 1)] > 0)

    used_prev2, used_prev, used_cur, used_next = (used_at(i - 2), used_at(i - 1), used_at(i),
                                                  used_at(i + 1))

    def gather_row(tile, r, buf, skew=0):
        row = src_ref[tile * tm + r + skew]
        pltpu.make_async_copy(x_hbm.at[pl.ds(row, 1), :],
                              xbuf.at[buf, pl.ds(r, 1), :], gsem.at[buf]).start()
        return row

    def scatter_row(tile, r, buf, skew=0):
        row = dst_ref[tile * tm + r + skew]
        pltpu.make_async_copy(obuf.at[buf, pl.ds(r, 1), :],
                              o_hbm.at[pl.ds(row, 1), :], ssem.at[buf]).start()
        return row

    def gather_loop(tile, buf):
        def body(r, carry):
            gather_row(tile, r, buf)
            return carry
        lax.fori_loop(0, tm, body, 0, unroll=8)

    def scatter_loop(tile, buf):
        def body(r, carry):
            scatter_row(tile, r, buf)
            return carry
        lax.fori_loop(0, tm, body, 0, unroll=8)

    def ffn():
        xb = xb_ref[...]
        gates = gs_ref[...]

        def expert(wg_ref, wu_ref, gate_col):
            gate = _dot(xb, wg_ref[0])
            up = _dot(xb, wu_ref[0])
            return ((gate * _sigmoid(gate)) * up * gate_col).astype(BF16)

        y_ref[...] = (_dot(expert(wga_ref, wua_ref, gates[:, 0:1]), wda_ref[0])
                      + _dot(expert(wgb_ref, wub_ref, gates[:, 1:2]), wdb_ref[0]))

    @pl.when(i == 0)
    def _():
        obuf[...] = jnp.zeros(obuf.shape, F32)
        for half in range(2):
            fill = pltpu.make_async_copy(obuf.at[half], o_hbm.at[pl.ds(n_rows_out + half * tm, tm), :],
                                         ssem.at[half])
            fill.start()
            fill.wait()

    @pl.when(jnp.logical_and(i == 0, used_cur))
    def _():
        gather_loop(0, 0)

    steady = jnp.logical_and(jnp.logical_and(used_prev, used_cur), used_next)

    @pl.when(steady)
    def _():
        for r in range(tm):
            gather_row(i + 1, r, other)

    @pl.when(used_cur)
    def _():
        pltpu.make_async_copy(x_hbm.at[pl.ds(0, tm), :], xbuf.at[cur], gsem.at[cur]).wait()
        xb_ref[...] = xbuf[cur].astype(BF16)

    @pl.when(steady)
    def _():
        for r in range(tm):
            scatter_row(i - 1, r, other)
        ffn()

    @pl.when(jnp.logical_not(steady))
    def _():
        @pl.when(used_next)
        def _():
            gather_loop(i + 1, other)

        @pl.when(used_prev)
        def _():
            scatter_loop(i - 1, other)

        @pl.when(used_cur)
        def _():
            ffn()

    @pl.when(used_prev2)
    def _():
        pltpu.make_async_copy(obuf.at[cur], o_hbm.at[pl.ds(0, tm), :], ssem.at[cur]).wait()

    @pl.when(used_cur)
    def _():
        obuf[cur] = _layer_norm(ALPHA * xbuf[cur] + y_ref[...], g_ref[...], b_ref[...])


def _moe(x, ri, rg, cnt, layer, wg, wu, wd, g, b, *, tm=256):
    t, d = x.shape
    n_tiles = t // tm + N_CLASSES + 2
    src_row, dst_row, gates, expert_a, expert_b, rows = _moe_plan(ri, rg, cnt, t, tm, n_tiles)
    w_in = lambda sel: pl.BlockSpec((None, 1, d, D_EXPERT),
                                    lambda i, s, dr, ea, eb, rows: (layer, sel(ea, eb)[i], 0, 0))
    w_out = lambda sel: pl.BlockSpec((None, 1, D_EXPERT, d),
                                     lambda i, s, dr, ea, eb, rows: (layer, sel(ea, eb)[i], 0, 0))
    pick_a = lambda ea, eb: ea
    pick_b = lambda ea, eb: eb
    grid_spec = pltpu.PrefetchScalarGridSpec(
        num_scalar_prefetch=5,
        grid=(n_tiles,),
        in_specs=[pl.BlockSpec(memory_space=pl.ANY),
                  pl.BlockSpec((tm, 2), lambda i, *_: (i, 0)),
                  w_in(pick_a), w_in(pick_a), w_out(pick_a),
                  w_in(pick_b), w_in(pick_b), w_out(pick_b),
                  pl.BlockSpec(g.shape, lambda i, *_: (0, 0)),
                  pl.BlockSpec(b.shape, lambda i, *_: (0, 0))],
        out_specs=pl.BlockSpec(memory_space=pl.ANY),
        scratch_shapes=[pltpu.VMEM((2, tm, d), F32), pltpu.VMEM((2, tm, d), F32),
                        pltpu.VMEM((tm, d), BF16), pltpu.VMEM((tm, d), F32),
                        pltpu.SemaphoreType.DMA((2,)), pltpu.SemaphoreType.DMA((2,))],
    )
    return pl.pallas_call(
        functools.partial(_moe_kernel, tm=tm, n_tiles=n_tiles, n_rows_out=t),
        out_shape=jax.ShapeDtypeStruct((t + 2 * tm, d), F32),
        grid_spec=grid_spec,
        compiler_params=_params(("arbitrary",)),
        name="moe_ln",
    )(src_row, dst_row, expert_a, expert_b, rows, x, gates, wg, wu, wd, wg, wu, wd, g, b)


def _rotate_half_cols(w):
    half = MLA_ROPE // 2
    return jnp.concatenate([-w[..., half:], w[..., :half]], axis=-1)


def _pack_even_in(w_in):
    cq, ckv, kr, gq, gk, gv, glr, go = jnp.split(
        w_in, np.cumsum([512, 512, 64, 512, 512, 1024, 16])[:].tolist(), axis=1)
    pad = jnp.zeros((w_in.shape[0], EVEN_IN_PAD - COL_GLR - GLA_GATE_RANK), w_in.dtype)
    return jnp.concatenate([cq, ckv, gq, gk, gv, go, kr, _rotate_half_cols(kr), glr, pad],
                           axis=1).astype(BF16)


def _pack_wq(w_uq):
    r = w_uq.shape[0]
    nope, rope = w_uq[..., :MLA_NOPE], w_uq[..., MLA_NOPE:]
    zeros = jnp.zeros((r, MLA_HEADS, MLA_QK_PAD - MLA_NOPE - MLA_ROPE), w_uq.dtype)
    wqa = jnp.concatenate([nope, rope, zeros], axis=-1).reshape(r, MLA_HEADS * MLA_QK_PAD)
    wqb = jnp.concatenate([_rotate_half_cols(rope), zeros], axis=-1).reshape(r, MLA_HEADS * LANE)
    return wqa.astype(BF16), wqb.astype(BF16)


def _pack_wkv(w_ukv):
    r = w_ukv.shape[0]
    kn = w_ukv[..., :MLA_NOPE].reshape(r, MLA_HEADS * MLA_NOPE)
    v = w_ukv[..., MLA_NOPE:].reshape(r, MLA_HEADS * MLA_V)
    return kn.astype(BF16), v.T.astype(BF16)


def kernel(x, positions, ev_w_in, ev_q_norm, ev_kv_norm, ev_w_uq, ev_w_ukv, ev_w_gk2, ev_b_gk2, ev_gla_norm, ev_w_out, od_w_in, od_sgu_ln_g, od_sgu_ln_b, od_w_s, od_b_s, od_w_out, ln_mix_g, ln_mix_b, ln_ffn_g, ln_ffn_b, router_w, router_b, moe_w_gate, moe_w_up, moe_w_down):
    batch, seq, d = x.shape
    t = batch * seq
    xf = x.reshape(t, d)
    pos = positions.reshape(t, 1)

    inv = 1.0 / (ROPE_THETA ** (jnp.arange(0, MLA_ROPE, 2, dtype=F32) / MLA_ROPE))
    inv_row = jnp.concatenate([inv, inv, jnp.zeros((LANE - MLA_ROPE,), F32)]).reshape(1, LANE)

    rw_t = router_w.T
    rw_hi = rw_t.astype(BF16)
    rw_lo = (rw_t - rw_hi.astype(F32)).astype(BF16)
    rw2 = jnp.concatenate([rw_hi, rw_lo], axis=0)
    rb = router_b.astype(F32).reshape(N_EXPERTS, 1)
    moe_wg, moe_wu, moe_wd = (w.astype(BF16) for w in (moe_w_gate, moe_w_up, moe_w_down))

    for i in range(DEPTH):
        j = i // 2
        if i % 2 == 0:
            h = _inproj(xf, _pack_even_in(ev_w_in[j]), t, gelu=False)
            wqa, wqb = _pack_wq(ev_w_uq[j])
            wkn, wvt = _pack_wkv(ev_w_ukv[j])
            q, kn, krot, vt = _mla_proj(h, pos, inv_row, ev_q_norm[j].reshape(1, -1),
                                        ev_kv_norm[j].reshape(1, -1), wqa, wqb, wkn, wvt)
            o_a = _attention(q, kn, krot, vt, batch, seq)
            wg2 = jnp.concatenate(
                [ev_w_gk2[j], jnp.zeros((LANE - GLA_GATE_RANK, GLA_HEADS * GLA_DK), F32)], axis=0)
            wg2_hi = wg2.astype(BF16)
            wg2_lo = (wg2 - wg2_hi.astype(F32)).astype(BF16)
            o_b = _gla(h, wg2_hi, wg2_lo, ev_b_gk2[j].reshape(1, -1),
                       ev_gla_norm[j].reshape(1, -1), batch, seq)
            w_out = ev_w_out[j].astype(BF16)
            split = MLA_HEADS * MLA_V
            acts, ws = [o_a, o_b], [w_out[:split], w_out[split:]]
        else:
            h = _inproj(xf, od_w_in[j].astype(BF16), t, gelu=True)
            bs_full = jnp.repeat(od_b_s[j].T, LANE, axis=1)
            gated = _sgu(h, od_sgu_ln_g[j].reshape(1, -1), od_sgu_ln_b[j].reshape(1, -1),
                         od_w_s[j], bs_full)
            acts, ws = [gated], [od_w_out[j].astype(BF16)]
        x1, ri, rg, cnt = _outproj(acts, ws, xf, ln_mix_g[i].reshape(1, -1),
                                   ln_mix_b[i].reshape(1, -1), rw2, rw_hi, rb)
        xf = _moe(x1, ri, rg, cnt, i, moe_wg, moe_wu, moe_wd,
                  ln_ffn_g[i].reshape(1, -1), ln_ffn_b[i].reshape(1, -1))
    return xf[:t].reshape(batch, seq, d)
```

```python
import functools

import jax
import jax.numpy as jnp
import numpy as np
from jax import lax
from jax.experimental import pallas as pl
from jax.experimental.pallas import tpu as pltpu

F32 = jnp.float32
BF16 = jnp.bfloat16

D_MODEL = 2048
DEPTH = 2
CHUNK = 64
MLA_HEADS = 8
MLA_RANK = 512
MLA_NOPE = 128
MLA_ROPE = 64
MLA_V = 128
MLA_QK_PAD = 256
MLA_VT_ROWS = 144
LOG2_E = 1.4426950408889634
ROPE_THETA = 10000.0
GLA_HEADS = 4
GLA_DK = 128
GLA_DV = 256
GLA_GATE_RANK = 16
GLA_GATE_NORM = 16.0
GLA_UNROLL = 8
SGU_BLOCK = 128
SGU_HALF = D_MODEL
SGU_GROUPS = 16
N_EXPERTS = 16
N_GROUPS = 4
EXPERTS_PER_GROUP = 4
D_EXPERT = 512
PAIR_SLOT_A = np.array([0, 0, 0, 1, 1, 3], np.int32)
PAIR_SLOT_B = np.array([1, 2, 3, 3, 2, 2], np.int32)
PAIRS_PER_GROUP = 6
N_CLASSES = N_GROUPS * PAIRS_PER_GROUP
N_OUT_BUFS = 3
CLASS_ROWS = 32
ROUTE_ROWS = 8
ALPHA = (2.0 * DEPTH) ** 0.25
EPS = 1e-6
NEG_BIG = -1e30

LANE = 128
VMEM_LIMIT = 56 * 1024 * 1024

COL_CQ, COL_CKV, COL_GQ, COL_GK = 0, 512, 1024, 1536
COL_GV, COL_GO, COL_KR, COL_GLR = 2048, 3072, 4096, 4224
EVEN_IN_PAD = 4352


def _dot(a, b):
    return jnp.dot(a, b, preferred_element_type=F32)


def _dot_nt(a, b):
    return lax.dot_general(a, b, (((1,), (1,)), ((), ())), preferred_element_type=F32)


def _dot_tn(a, b):
    return lax.dot_general(a, b, (((0,), (0,)), ((), ())), preferred_element_type=F32)


def _split_hi_lo(x):
    hi = x.astype(BF16)
    lo = (x - hi.astype(F32)).astype(BF16)
    return hi, lo


def _layer_norm(y, g, b):
    mu = jnp.mean(y, axis=-1, keepdims=True)
    d = y - mu
    var = jnp.mean(d * d, axis=-1, keepdims=True)
    return d * lax.rsqrt(var + EPS) * g + b


def _rms_norm(y, g):
    return y * lax.rsqrt(jnp.mean(y * y, axis=-1, keepdims=True) + EPS) * g


def _sigmoid(x):
    return 1.0 / (1.0 + jnp.exp(-x))


def _params(sem):
    return pltpu.CompilerParams(dimension_semantics=sem, vmem_limit_bytes=VMEM_LIMIT)


def _resident(shape):
    nd = len(shape)
    return pl.BlockSpec(shape, lambda *_: (0,) * nd, pipeline_mode=pl.Buffered(1))


def _inproj_kernel(x_ref, w_ref, o_ref, *, gelu, n_chunk):
    xb = x_ref[...].astype(BF16)
    n = w_ref.shape[1]
    for n0 in range(0, n, n_chunk):
        nw = min(n_chunk, n - n0)
        y = _dot(xb, w_ref[:, n0:n0 + nw])
        if gelu:
            y = 0.5 * y * (1.0 + jnp.tanh(0.7978845608028654 * (y + 0.044715 * (y * y * y))))
        o_ref[:, n0:n0 + nw] = y.astype(o_ref.dtype)


def _inproj(x, w, t, *, gelu, tm=512):
    d = x.shape[1]
    n = w.shape[1]
    return pl.pallas_call(
        functools.partial(_inproj_kernel, gelu=gelu, n_chunk=512),
        out_shape=jax.ShapeDtypeStruct((t, n), BF16),
        grid=(t // tm,),
        in_specs=[pl.BlockSpec((tm, d), lambda i: (i, 0)), _resident((d, n))],
        out_specs=pl.BlockSpec((tm, n), lambda i: (i, 0)),
        compiler_params=_params(("parallel",)),
        name="inproj_gelu" if gelu else "inproj",
    )(x, w)


def _mla_proj_kernel(cq_ref, ckv_ref, kr_ref, pos_ref, inv_ref, gq_ref, gkv_ref,
                     wqa_ref, wqb_ref, wkn_ref, wvt_ref, q_ref, kn_ref, krot_ref, vt_ref):
    scale = (MLA_NOPE + MLA_ROPE) ** -0.5 * LOG2_E
    ang = pos_ref[...].astype(F32) * inv_ref[...]
    lane = lax.broadcasted_iota(jnp.int32, ang.shape, 1)
    rot_lane = lane < MLA_ROPE
    cos = jnp.where(rot_lane, jnp.cos(ang), 0.0)
    sin = jnp.where(rot_lane, jnp.sin(ang), 0.0)

    nq = _rms_norm(cq_ref[...].astype(F32), gq_ref[...]).astype(BF16)
    for h in range(MLA_HEADS):
        a = _dot(nq, wqa_ref[:, h * MLA_QK_PAD:(h + 1) * MLA_QK_PAD])
        b = _dot(nq, wqb_ref[:, h * LANE:(h + 1) * LANE])
        q_ref[:, h * MLA_QK_PAD:h * MLA_QK_PAD + LANE] = (a[:, :LANE] * scale).astype(BF16)
        q_ref[:, h * MLA_QK_PAD + LANE:(h + 1) * MLA_QK_PAD] = (
            (a[:, LANE:] * cos + b * sin) * scale).astype(BF16)

    nkv = _rms_norm(ckv_ref[...].astype(F32), gkv_ref[...]).astype(BF16)
    kn_ref[...] = _dot(nkv, wkn_ref[...]).astype(BF16)
    tm = nkv.shape[0]
    tail_row = lax.broadcasted_iota(jnp.int32, (MLA_VT_ROWS - MLA_V, tm), 0)
    tail = jnp.where(tail_row == 0, 1.0, 0.0).astype(BF16)
    for h in range(MLA_HEADS):
        vt = _dot_nt(wvt_ref[h * MLA_V:(h + 1) * MLA_V, :], nkv)
        vt_ref[h * MLA_VT_ROWS:h * MLA_VT_ROWS + MLA_V, :] = vt.astype(BF16)
        vt_ref[h * MLA_VT_ROWS + MLA_V:(h + 1) * MLA_VT_ROWS, :] = tail

    kr = kr_ref[...].astype(F32)
    krot_ref[...] = (kr * cos + pltpu.roll(kr, MLA_ROPE, axis=1) * sin).astype(BF16)


def _mla_proj(h, pos, inv, gq, gkv, wqa, wqb, wkn, wvt, *, tm=512):
    t = h.shape[0]
    hw = MLA_HEADS * MLA_NOPE
    return pl.pallas_call(
        _mla_proj_kernel,
        out_shape=(jax.ShapeDtypeStruct((t, MLA_HEADS * MLA_QK_PAD), BF16),
                   jax.ShapeDtypeStruct((t, hw), BF16),
                   jax.ShapeDtypeStruct((t, LANE), BF16),
                   jax.ShapeDtypeStruct((MLA_HEADS * MLA_VT_ROWS, t), BF16)),
        grid=(t // tm,),
        in_specs=[pl.BlockSpec((tm, MLA_RANK), lambda i: (i, COL_CQ // MLA_RANK)),
                  pl.BlockSpec((tm, MLA_RANK), lambda i: (i, COL_CKV // MLA_RANK)),
                  pl.BlockSpec((tm, LANE), lambda i: (i, COL_KR // LANE)),
                  pl.BlockSpec((tm, 1), lambda i: (i, 0)),
                  _resident((1, LANE)), _resident((1, MLA_RANK)), _resident((1, MLA_RANK)),
                  _resident(wqa.shape), _resident(wqb.shape), _resident(wkn.shape),
                  _resident(wvt.shape)],
        out_specs=(pl.BlockSpec((tm, MLA_HEADS * MLA_QK_PAD), lambda i: (i, 0)),
                   pl.BlockSpec((tm, hw), lambda i: (i, 0)),
                   pl.BlockSpec((tm, LANE), lambda i: (i, 0)),
                   pl.BlockSpec((MLA_HEADS * MLA_VT_ROWS, tm), lambda i: (0, i))),
        compiler_params=_params(("parallel",)),
        name="mla_proj",
    )(h, h, h, pos, inv, gq, gkv, wqa, wqb, wkn, wvt)


def _col_max(x):
    rows, cols = x.shape
    part = jnp.max(x.reshape(rows // 64, 64, cols), axis=0)
    return jnp.max(part, axis=0, keepdims=True)


def _attn_kernel(qi_ref, ki_ref, last_ref, q_ref, kn_ref, krot_ref, vt_ref, o_ref, m_ref, acc_ref,
                 *, tq, tk):
    p_id = pl.program_id(1)
    qi = qi_ref[p_id]
    ki = ki_ref[p_id]
    is_last = last_ref[p_id] == 1

    @pl.when(ki == 0)
    def _():
        m_ref[...] = jnp.full(m_ref.shape, NEG_BIG, F32)
        acc_ref[...] = jnp.zeros(acc_ref.shape, F32)

    def step(masked):
        krot = krot_ref[...]
        if masked:
            k_chunk = ki * (tk // CHUNK) + lax.broadcasted_iota(jnp.int32, (tk, tq), 0) // CHUNK
            q_chunk = qi * (tq // CHUNK) + lax.broadcasted_iota(jnp.int32, (tk, tq), 1) // CHUNK
            visible = k_chunk <= q_chunk
        def scores(h):
            kh = jnp.concatenate([kn_ref[:, h * MLA_NOPE:(h + 1) * MLA_NOPE], krot], axis=1)
            return _dot_nt(kh, q_ref[:, h * MLA_QK_PAD:(h + 1) * MLA_QK_PAD])

        st_next = scores(0)
        for h in range(MLA_HEADS):
            st = st_next
            if h + 1 < MLA_HEADS:
                st_next = scores(h + 1)
            if masked:
                st = jnp.where(visible, st, NEG_BIG)
            m_prev = m_ref[h]
            m_new = jnp.maximum(m_prev, _col_max(st)).astype(BF16)
            alpha = jnp.exp2(m_prev - m_new.astype(F32))
            p = jnp.exp2(st.astype(BF16) - m_new)
            m_new = m_new.astype(F32)
            m_ref[h] = m_new
            acc_ref[h] = alpha * acc_ref[h] + _dot(vt_ref[h * MLA_VT_ROWS:(h + 1) * MLA_VT_ROWS, :], p)

    @pl.when(jnp.logical_not(is_last))
    def _():
        step(False)

    @pl.when(is_last)
    def _():
        step(True)
        for h in range(MLA_HEADS):
            acc = acc_ref[h]
            out_t = acc[:MLA_V, :] / acc[MLA_V:MLA_V + 1, :]
            o_ref[:, h * MLA_V:(h + 1) * MLA_V] = out_t.T.astype(o_ref.dtype)


def _attn_pairs(seq, tq, tk):
    qi, ki, last = [], [], []
    for i in range(seq // tq):
        n_kv = ((i + 1) * tq - 1) // tk + 1
        for j in range(n_kv):
            qi.append(i)
            ki.append(j)
            last.append(1 if j == n_kv - 1 else 0)
    return np.asarray(qi, np.int32), np.asarray(ki, np.int32), np.asarray(last, np.int32)


def _attention(q, kn, krot, vt, batch, seq, *, tq=512, tk=1024):
    qi, ki, last = _attn_pairs(seq, tq, tk)
    nq, nk = seq // tq, seq // tk
    hw = MLA_HEADS * MLA_V
    grid_spec = pltpu.PrefetchScalarGridSpec(
        num_scalar_prefetch=3,
        grid=(batch, len(qi)),
        in_specs=[
            pl.BlockSpec((tq, MLA_HEADS * MLA_QK_PAD), lambda b, p, qi, ki, la: (b * nq + qi[p], 0)),
            pl.BlockSpec((tk, hw), lambda b, p, qi, ki, la: (b * nk + ki[p], 0)),
            pl.BlockSpec((tk, LANE), lambda b, p, qi, ki, la: (b * nk + ki[p], 0)),
            pl.BlockSpec((MLA_HEADS * MLA_VT_ROWS, tk), lambda b, p, qi, ki, la: (0, b * nk + ki[p])),
        ],
        out_specs=pl.BlockSpec((tq, hw), lambda b, p, qi, ki, la: (b * nq + qi[p], 0)),
        scratch_shapes=[pltpu.VMEM((MLA_HEADS, 1, tq), F32),
                        pltpu.VMEM((MLA_HEADS, MLA_VT_ROWS, tq), F32)],
    )
    return pl.pallas_call(
        functools.partial(_attn_kernel, tq=tq, tk=tk),
        out_shape=jax.ShapeDtypeStruct((batch * seq, hw), BF16),
        grid_spec=grid_spec,
        compiler_params=_params(("parallel", "arbitrary")),
        name="mla_attention",
    )(jnp.asarray(qi), jnp.asarray(ki), jnp.asarray(last), q, kn, krot, vt)


def _gla_kernel(gq_ref, gk_ref, gv_ref, go_ref, glr_ref, whi_ref, wlo_ref, bg_ref, gn_ref,
                o_ref, st_ref, *, n_chunks):
    @pl.when(pl.program_id(1) == 0)
    def _():
        st_ref[...] = jnp.zeros(st_ref.shape, F32)

    row = lax.broadcasted_iota(jnp.int32, (CHUNK, CHUNK), 0)
    col = lax.broadcasted_iota(jnp.int32, (CHUNK, CHUNK), 1)
    tril = jnp.where(col <= row, 1.0, 0.0).astype(BF16)
    q_scale = GLA_DK ** -0.5

    def chunk_body(c, carry):
        rows = pl.ds(pl.multiple_of(c * CHUNK, CHUNK), CHUNK)
        glr = glr_ref[rows, :]
        logit = _dot(glr, whi_ref[...]) + _dot(glr, wlo_ref[...]) + bg_ref[...]
        lg = -(jnp.maximum(-logit, 0.0) + jnp.log1p(jnp.exp(-jnp.abs(logit)))) / GLA_GATE_NORM
        lg_hi, lg_lo = _split_hi_lo(lg)
        cum = _dot(tril, lg_hi) + _dot(tril, lg_lo)
        cum_end = cum[CHUNK - 1:CHUNK, :]
        k_dec = (gk_ref[rows, :].astype(F32) * jnp.exp(cum_end - cum)).astype(BF16)
        decay = jnp.exp(cum_end)
        qs = (gq_ref[rows, :].astype(F32) * q_scale).astype(BF16)
        for h in range(GLA_HEADS):
            ks = slice(h * GLA_DK, (h + 1) * GLA_DK)
            vs = slice(h * GLA_DV, (h + 1) * GLA_DV)
            upd_t = _dot_tn(gv_ref[rows, vs], k_dec[:, ks])
            st = st_ref[h] * decay[:, ks] + upd_t
            st_ref[h] = st
            o = _dot_nt(qs[:, ks], st.astype(BF16))
            o = _rms_norm(o, gn_ref[...])
            g = go_ref[rows, vs].astype(F32)
            o_ref[rows, vs] = (o * (g * _sigmoid(g))).astype(o_ref.dtype)
        return carry

    lax.fori_loop(0, n_chunks, chunk_body, 0, unroll=GLA_UNROLL)


def _gla(h, whi, wlo, bg, gn, batch, seq, *, tc=512):
    t = h.shape[0]
    nb = seq // tc
    kw, vw = GLA_HEADS * GLA_DK, GLA_HEADS * GLA_DV
    return pl.pallas_call(
        functools.partial(_gla_kernel, n_chunks=tc // CHUNK),
        out_shape=jax.ShapeDtypeStruct((t, vw), BF16),
        grid=(batch, nb),
        in_specs=[pl.BlockSpec((tc, kw), lambda b, i: (b * nb + i, COL_GQ // kw)),
                  pl.BlockSpec((tc, kw), lambda b, i: (b * nb + i, COL_GK // kw)),
                  pl.BlockSpec((tc, vw), lambda b, i: (b * nb + i, COL_GV // vw)),
                  pl.BlockSpec((tc, vw), lambda b, i: (b * nb + i, COL_GO // vw)),
                  pl.BlockSpec((tc, LANE), lambda b, i: (b * nb + i, COL_GLR // LANE)),
                  _resident(whi.shape), _resident(wlo.shape), _resident(bg.shape),
                  _resident(gn.shape)],
        out_specs=pl.BlockSpec((tc, vw), lambda b, i: (b * nb + i, 0)),
        scratch_shapes=[pltpu.VMEM((GLA_HEADS, GLA_DV, GLA_DK), F32)],
        compiler_params=_params(("parallel", "arbitrary")),
        name="gla",
    )(h, h, h, h, h, whi, wlo, bg, gn)


def _sgu_kernel(u_ref, v_ref, lg_ref, lb_ref, ws_ref, bs_ref, o_ref):
    tm = u_ref.shape[0]
    vn = _layer_norm(v_ref[...].astype(F32), lg_ref[...], lb_ref[...]).astype(BF16)
    pos_chunk_i = lax.broadcasted_iota(jnp.int32, (SGU_BLOCK, SGU_BLOCK), 0) // CHUNK
    pos_chunk_j = lax.broadcasted_iota(jnp.int32, (SGU_BLOCK, SGU_BLOCK), 1) // CHUNK
    causal = pos_chunk_j <= pos_chunk_i
    for g in range(SGU_GROUPS):
        cs = slice(g * LANE, (g + 1) * LANE)
        w = jnp.where(causal, ws_ref[g], 0.0).astype(BF16)
        for n in range(tm // SGU_BLOCK):
            rs = slice(n * SGU_BLOCK, (n + 1) * SGU_BLOCK)
            mixed = _dot(w, vn[rs, cs]) + bs_ref[:, cs]
            o_ref[rs, cs] = (u_ref[rs, cs].astype(F32) * mixed).astype(o_ref.dtype)


def _sgu(h, lg, lb, ws, bs_full, *, tm=256):
    t = h.shape[0]
    return pl.pallas_call(
        _sgu_kernel,
        out_shape=jax.ShapeDtypeStruct((t, SGU_HALF), BF16),
        grid=(t // tm,),
        in_specs=[pl.BlockSpec((tm, SGU_HALF), lambda i: (i, 0)),
                  pl.BlockSpec((tm, SGU_HALF), lambda i: (i, 1)),
                  _resident(lg.shape), _resident(lb.shape), _resident(ws.shape),
                  _resident(bs_full.shape)],
        out_specs=pl.BlockSpec((tm, SGU_HALF), lambda i: (i, 0)),
        compiler_params=_params(("parallel",)),
        name="sgu",
    )(h, h, lg, lb, ws, bs_full)


def _route_logits(x1, rw2_ref, rwhi_ref):
    x_hi, x_lo = _split_hi_lo(x1)
    part = _dot_nt(rw2_ref[...], x_hi)
    return part[:N_EXPERTS] + part[N_EXPERTS:] + _dot_nt(rwhi_ref[...], x_lo)


def _route_select(logits, rb_ref):
    scores = _sigmoid(logits)
    biased = scores + rb_ref[...]
    b = [biased[e:e + 1, :] for e in range(N_EXPERTS)]
    s = [scores[e:e + 1, :] for e in range(N_EXPERTS)]

    def top2_sum(v):
        best = v[0] + v[1]
        for i in range(len(v)):
            for j in range(i + 1, len(v)):
                if (i, j) != (0, 1):
                    best = jnp.maximum(best, v[i] + v[j])
        return best

    def argmax_first(v):
        best_v, best_i = v[0], jnp.zeros(v[0].shape, jnp.int32)
        for i in range(1, len(v)):
            upd = v[i] > best_v
            best_i = jnp.where(upd, i, best_i)
            best_v = jnp.where(upd, v[i], best_v)
        return best_i

    def pick(v, idx):
        out = v[len(v) - 1]
        for i in range(len(v) - 2, -1, -1):
            out = jnp.where(idx == i, v[i], out)
        return out

    grp = [top2_sum(b[g * EXPERTS_PER_GROUP:(g + 1) * EXPERTS_PER_GROUP]) for g in range(N_GROUPS)]
    best_g = argmax_first(grp)
    cand_b = [pick([b[g * EXPERTS_PER_GROUP + j] for g in range(N_GROUPS)], best_g)
              for j in range(EXPERTS_PER_GROUP)]
    cand_s = [pick([s[g * EXPERTS_PER_GROUP + j] for g in range(N_GROUPS)], best_g)
              for j in range(EXPERTS_PER_GROUP)]
    i1 = argmax_first(cand_b)
    cand_b2 = [jnp.where(i1 == j, -jnp.inf, cand_b[j]) for j in range(EXPERTS_PER_GROUP)]
    i2 = argmax_first(cand_b2)
    g1 = pick(cand_s, i1)
    g2 = pick(cand_s, i2)
    den = g1 + g2
    w1, w2 = g1 / den, g2 / den
    lo, hi = jnp.minimum(i1, i2), jnp.maximum(i1, i2)
    pair = jnp.where(lo == 0, hi - 1, jnp.where(lo == 1, jnp.where(hi == 3, 3, 4), 5))
    slot_a = jnp.where(pair < 3, 0, jnp.where(pair < 5, 1, 3))
    first_is_a = i1 == slot_a
    cls = best_g * PAIRS_PER_GROUP + pair
    return cls, jnp.where(first_is_a, w1, w2), jnp.where(first_is_a, w2, w1)


def _outproj_kernel(*refs, n_in):
    a_refs = refs[:n_in]
    w_refs = refs[n_in:2 * n_in]
    (x_ref, g_ref, b_ref, rw2_ref, rwhi_ref, rb_ref,
     x1_ref, ri_ref, rg_ref, cnt_ref, carry_ref, prev_ref) = refs[2 * n_in:]
    tm = x_ref.shape[0]
    i = pl.program_id(0)

    @pl.when(i == 0)
    def _():
        carry_ref[...] = jnp.zeros(carry_ref.shape, F32)
        prev_ref[...] = jnp.zeros(prev_ref.shape, F32)

    counted = jnp.where(i > 0, 1.0, 0.0)
    d_out = x_ref.shape[1]
    n_chunk = d_out // 4

    def project(c):
        cols = slice(c * n_chunk, (c + 1) * n_chunk)
        acc = _dot(a_refs[0][...], w_refs[0][:, cols])
        for a_ref, w_ref in zip(a_refs[1:], w_refs[1:]):
            acc = acc + _dot(a_ref[...], w_ref[:, cols])
        return acc

    mix = [project(0)]
    logits = _route_logits(prev_ref[...], rw2_ref, rwhi_ref)
    mix.append(project(1))
    cls, gate_a, gate_b = _route_select(logits, rb_ref)
    class_row = lax.broadcasted_iota(jnp.int32, (CLASS_ROWS, tm), 0)
    onehot = jnp.where(class_row == cls, 1.0, 0.0)
    t_prev = lax.broadcasted_iota(jnp.int32, (tm, tm), 0)
    t_cur = lax.broadcasted_iota(jnp.int32, (tm, tm), 1)
    before = jnp.where(t_prev < t_cur, 1.0, 0.0).astype(BF16)
    mix.append(project(2))
    seen = _dot(onehot.astype(BF16), before) + carry_ref[...]
    mix.append(project(3))
    rank = jnp.sum(onehot * seen, axis=0, keepdims=True)
    carry_ref[...] += counted * jnp.sum(onehot, axis=1, keepdims=True)

    pad_i = jnp.zeros((ROUTE_ROWS - 2, tm), jnp.int32)
    ri_ref[...] = jnp.concatenate([cls, rank.astype(jnp.int32), pad_i], axis=0)
    rg_ref[...] = jnp.concatenate([gate_a, gate_b, pad_i.astype(F32)], axis=0)
    cnt_ref[...] = jnp.broadcast_to(carry_ref[...], cnt_ref.shape).astype(jnp.int32)

    x1 = _layer_norm(ALPHA * x_ref[...] + jnp.concatenate(mix, axis=1), g_ref[...], b_ref[...])
    x1_ref[...] = x1
    prev_ref[...] = x1


def _outproj(acts, ws, x, g, b, rw2, rwhi, rb, *, tm=256):
    t, d = acts[0].shape[0], x.shape[1]
    n_in = len(acts)
    n = t // tm
    cur = lambda i: (jnp.minimum(i, n - 1), 0)
    routed = lambda i: (0, jnp.maximum(i - 1, 0))
    in_specs = ([pl.BlockSpec((tm, a.shape[1]), cur) for a in acts]
                + [_resident(w.shape) for w in ws]
                + [pl.BlockSpec((tm, d), cur),
                   _resident(g.shape), _resident(b.shape), _resident(rw2.shape),
                   _resident(rwhi.shape), _resident(rb.shape)])
    return pl.pallas_call(
        functools.partial(_outproj_kernel, n_in=n_in),
        out_shape=(jax.ShapeDtypeStruct((t, d), F32),
                   jax.ShapeDtypeStruct((ROUTE_ROWS, t), jnp.int32),
                   jax.ShapeDtypeStruct((ROUTE_ROWS, t), F32),
                   jax.ShapeDtypeStruct((CLASS_ROWS, LANE), jnp.int32)),
        grid=(n + 1,),
        in_specs=in_specs,
        out_specs=(pl.BlockSpec((tm, d), cur),
                   pl.BlockSpec((ROUTE_ROWS, tm), routed),
                   pl.BlockSpec((ROUTE_ROWS, tm), routed),
                   pl.BlockSpec((CLASS_ROWS, LANE), lambda i: (0, 0))),
        scratch_shapes=[pltpu.VMEM((CLASS_ROWS, 1), F32), pltpu.VMEM((tm, d), F32)],
        compiler_params=_params(("arbitrary",)),
        name="outproj_ln_route",
    )(*acts, *ws, x, g, b, rw2, rwhi, rb)


def _moe_plan(ri, rg, cnt, t, tm, n_tiles):
    cls, rank = ri[0], ri[1]
    counts = cnt[:N_CLASSES, 0]
    tiles_c = (counts + tm - 1) // tm
    tile_end = jnp.cumsum(tiles_c)
    tile_start = tile_end - tiles_c
    used = tile_end[-1]
    slot = tile_start[cls] * tm + rank
    per_token = jnp.stack([jnp.arange(1, t + 1, dtype=F32), rg[0], rg[1]], axis=1)
    per_slot = jnp.zeros((n_tiles * tm, 3), F32).at[slot].set(per_token, unique_indices=True)
    gates = per_slot[:, 1:]
    row = jnp.maximum(per_slot[:, 0].astype(jnp.int32) - 1, 0)
    tile = jnp.arange(n_tiles, dtype=jnp.int32)
    covered = tile_end[None, :] <= jnp.minimum(tile, used - 1)[:, None]
    tile_cls = jnp.minimum(jnp.sum(covered.astype(jnp.int32), axis=1), N_CLASSES - 1)
    rows = jnp.clip(counts[tile_cls] - (tile - tile_start[tile_cls]) * tm, 0, tm)
    rows = jnp.where(tile < used, rows, 0).astype(jnp.int32)
    grp, pair = tile_cls // PAIRS_PER_GROUP, tile_cls % PAIRS_PER_GROUP
    expert_a = (grp * EXPERTS_PER_GROUP + jnp.asarray(PAIR_SLOT_A)[pair]).astype(jnp.int32)
    expert_b = (grp * EXPERTS_PER_GROUP + jnp.asarray(PAIR_SLOT_B)[pair]).astype(jnp.int32)
    return row, gates, expert_a, expert_b, rows


def _moe_kernel(row_ref, ea_ref, eb_ref, rows_ref, x_hbm, gs_ref,
                wga_ref, wua_ref, wda_ref, wgb_ref, wub_ref, wdb_ref, g_ref, b_ref,
                o_hbm, xbuf, obuf, xb_ref, gsem, ssem, *, tm, n_tiles):
    i = pl.program_id(0)
    cur = i % 2
    other = 1 - cur
    ocur = i % N_OUT_BUFS
    oprev = (i + N_OUT_BUFS - 1) % N_OUT_BUFS

    def rows_at(tile):
        inside = jnp.logical_and(tile >= 0, tile < n_tiles)
        return jnp.where(inside, rows_ref[jnp.clip(tile, 0, n_tiles - 1)], 0)

    n_oldest, n_prev, n_cur, n_next = (rows_at(i - N_OUT_BUFS), rows_at(i - 1), rows_at(i),
                                       rows_at(i + 1))

    def gather_row(tile, r, buf):
        pltpu.make_async_copy(x_hbm.at[pl.ds(row_ref[tile * tm + r], 1), :],
                              xbuf.at[buf, pl.ds(r, 1), :], gsem.at[buf]).start()

    def scatter_row(tile, r, buf):
        pltpu.make_async_copy(obuf.at[buf, pl.ds(r, 1), :],
                              o_hbm.at[pl.ds(row_ref[tile * tm + r], 1), :], ssem.at[buf]).start()

    def ffn_ln():
        xb = xb_ref[...]
        gates = gs_ref[...]

        def expert(wg_ref, wu_ref, gate_col):
            gate = _dot(xb, wg_ref[0])
            up = _dot(xb, wu_ref[0])
            return ((gate * _sigmoid(gate)) * up * gate_col).astype(BF16)

        y = (_dot(expert(wga_ref, wua_ref, gates[:, 0:1]), wda_ref[0])
             + _dot(expert(wgb_ref, wub_ref, gates[:, 1:2]), wdb_ref[0]))
        obuf[ocur] = _layer_norm(ALPHA * xbuf[cur] + y, g_ref[...], b_ref[...])

    @pl.when(jnp.logical_and(i == 0, n_cur > 0))
    def _():
        def body(r, carry):
            gather_row(0, r, 0)
            return carry
        lax.fori_loop(0, tm, body, 0, unroll=8)

    @pl.when(n_next > 0)
    def _():
        for r in range(tm):
            gather_row(i + 1, r, other)

    @pl.when(n_cur > 0)
    def _():
        pltpu.make_async_copy(x_hbm.at[pl.ds(0, tm), :], xbuf.at[cur], gsem.at[cur]).wait()
        xb_ref[...] = xbuf[cur].astype(BF16)

    @pl.when(n_oldest > 0)
    def _():
        bit = 1
        while bit <= tm:
            @pl.when((n_oldest & bit) != 0)
            def _(bit=bit):
                pltpu.make_async_copy(obuf.at[ocur, pl.ds(0, bit), :],
                                      o_hbm.at[pl.ds(0, bit), :], ssem.at[ocur]).wait()
            bit *= 2

    fast = jnp.logical_and(n_prev == tm, n_cur > 0)

    @pl.when(fast)
    def _():
        for r in range(tm):
            scatter_row(i - 1, r, oprev)
        ffn_ln()

    @pl.when(jnp.logical_not(fast))
    def _():
        @pl.when(n_prev > 0)
        def _():
            def body(r, carry):
                scatter_row(i - 1, r, oprev)
                return carry
            lax.fori_loop(0, n_prev, body, 0)

        @pl.when(n_cur > 0)
        def _():
            ffn_ln()


def _moe(x, ri, rg, cnt, layer, wg, wu, wd, g, b, *, tm=256):
    t, d = x.shape
    n_tiles = t // tm + N_CLASSES + N_OUT_BUFS
    row, gates, expert_a, expert_b, rows = _moe_plan(ri, rg, cnt, t, tm, n_tiles)
    w_in = lambda sel: pl.BlockSpec((None, 1, d, D_EXPERT),
                                    lambda i, row, ea, eb, rows: (layer, sel(ea, eb)[i], 0, 0))
    w_out = lambda sel: pl.BlockSpec((None, 1, D_EXPERT, d),
                                     lambda i, row, ea, eb, rows: (layer, sel(ea, eb)[i], 0, 0))
    pick_a = lambda ea, eb: ea
    pick_b = lambda ea, eb: eb
    grid_spec = pltpu.PrefetchScalarGridSpec(
        num_scalar_prefetch=4,
        grid=(n_tiles,),
        in_specs=[pl.BlockSpec(memory_space=pl.ANY),
                  pl.BlockSpec((tm, 2), lambda i, *_: (i, 0)),
                  w_in(pick_a), w_in(pick_a), w_out(pick_a),
                  w_in(pick_b), w_in(pick_b), w_out(pick_b),
                  pl.BlockSpec(g.shape, lambda i, *_: (0, 0)),
                  pl.BlockSpec(b.shape, lambda i, *_: (0, 0))],
        out_specs=pl.BlockSpec(memory_space=pl.ANY),
        scratch_shapes=[pltpu.VMEM((2, tm, d), F32), pltpu.VMEM((N_OUT_BUFS, tm, d), F32),
                        pltpu.VMEM((tm, d), BF16),
                        pltpu.SemaphoreType.DMA((2,)), pltpu.SemaphoreType.DMA((N_OUT_BUFS,))],
    )
    return pl.pallas_call(
        functools.partial(_moe_kernel, tm=tm, n_tiles=n_tiles),
        out_shape=jax.ShapeDtypeStruct((t, d), F32),
        grid_spec=grid_spec,
        compiler_params=_params(("arbitrary",)),
        name="moe_ln",
    )(row, expert_a, expert_b, rows, x, gates, wg, wu, wd, wg, wu, wd, g, b)


def _rotate_half_cols(w):
    half = MLA_ROPE // 2
    return jnp.concatenate([-w[..., half:], w[..., :half]], axis=-1)


def _pack_even_in(w_in):
    cq, ckv, kr, gq, gk, gv, glr, go = jnp.split(
        w_in, np.cumsum([512, 512, 64, 512, 512, 1024, 16])[:].tolist(), axis=1)
    pad = jnp.zeros((w_in.shape[0], EVEN_IN_PAD - COL_GLR - GLA_GATE_RANK), w_in.dtype)
    return jnp.concatenate([cq, ckv, gq, gk, gv, go, kr, _rotate_half_cols(kr), glr, pad],
                           axis=1).astype(BF16)


def _pack_wq(w_uq):
    r = w_uq.shape[0]
    nope, rope = w_uq[..., :MLA_NOPE], w_uq[..., MLA_NOPE:]
    zeros = jnp.zeros((r, MLA_HEADS, MLA_QK_PAD - MLA_NOPE - MLA_ROPE), w_uq.dtype)
    wqa = jnp.concatenate([nope, rope, zeros], axis=-1).reshape(r, MLA_HEADS * MLA_QK_PAD)
    wqb = jnp.concatenate([_rotate_half_cols(rope), zeros], axis=-1).reshape(r, MLA_HEADS * LANE)
    return wqa.astype(BF16), wqb.astype(BF16)


def _pack_wkv(w_ukv):
    r = w_ukv.shape[0]
    kn = w_ukv[..., :MLA_NOPE].reshape(r, MLA_HEADS * MLA_NOPE)
    v = w_ukv[..., MLA_NOPE:].reshape(r, MLA_HEADS * MLA_V)
    return kn.astype(BF16), v.T.astype(BF16)


def kernel(x, positions, ev_w_in, ev_q_norm, ev_kv_norm, ev_w_uq, ev_w_ukv, ev_w_gk2, ev_b_gk2, ev_gla_norm, ev_w_out, od_w_in, od_sgu_ln_g, od_sgu_ln_b, od_w_s, od_b_s, od_w_out, ln_mix_g, ln_mix_b, ln_ffn_g, ln_ffn_b, router_w, router_b, moe_w_gate, moe_w_up, moe_w_down):
    batch, seq, d = x.shape
    t = batch * seq
    xf = x.reshape(t, d)
    pos = positions.reshape(t, 1)

    inv = 1.0 / (ROPE_THETA ** (jnp.arange(0, MLA_ROPE, 2, dtype=F32) / MLA_ROPE))
    inv_row = jnp.concatenate([inv, inv, jnp.zeros((LANE - MLA_ROPE,), F32)]).reshape(1, LANE)

    rw_t = router_w.T
    rw_hi = rw_t.astype(BF16)
    rw_lo = (rw_t - rw_hi.astype(F32)).astype(BF16)
    rw2 = jnp.concatenate([rw_hi, rw_lo], axis=0)
    rb = router_b.astype(F32).reshape(N_EXPERTS, 1)
    moe_wg, moe_wu, moe_wd = (w.astype(BF16) for w in (moe_w_gate, moe_w_up, moe_w_down))

    for i in range(DEPTH):
        j = i // 2
        if i % 2 == 0:
            h = _inproj(xf, _pack_even_in(ev_w_in[j]), t, gelu=False)
            wqa, wqb = _pack_wq(ev_w_uq[j])
            wkn, wvt = _pack_wkv(ev_w_ukv[j])
            q, kn, krot, vt = _mla_proj(h, pos, inv_row, ev_q_norm[j].reshape(1, -1),
                                        ev_kv_norm[j].reshape(1, -1), wqa, wqb, wkn, wvt)
            o_a = _attention(q, kn, krot, vt, batch, seq)
            wg2 = jnp.concatenate(
                [ev_w_gk2[j], jnp.zeros((LANE - GLA_GATE_RANK, GLA_HEADS * GLA_DK), F32)], axis=0)
            wg2_hi = wg2.astype(BF16)
            wg2_lo = (wg2 - wg2_hi.astype(F32)).astype(BF16)
            o_b = _gla(h, wg2_hi, wg2_lo, ev_b_gk2[j].reshape(1, -1),
                       ev_gla_norm[j].reshape(1, -1), batch, seq)
            w_out = ev_w_out[j].astype(BF16)
            split = MLA_HEADS * MLA_V
            acts, ws = [o_a, o_b], [w_out[:split], w_out[split:]]
        else:
            h = _inproj(xf, od_w_in[j].astype(BF16), t, gelu=True)
            bs_full = jnp.repeat(od_b_s[j].T, LANE, axis=1)
            gated = _sgu(h, od_sgu_ln_g[j].reshape(1, -1), od_sgu_ln_b[j].reshape(1, -1),
                         od_w_s[j], bs_full)
            acts, ws = [gated], [od_w_out[j].astype(BF16)]
        x1, ri, rg, cnt = _outproj(acts, ws, xf, ln_mix_g[i].reshape(1, -1),
                                   ln_mix_b[i].reshape(1, -1), rw2, rw_hi, rb)
        xf = _moe(x1, ri, rg, cnt, i, moe_wg, moe_wu, moe_wd,
                  ln_ffn_g[i].reshape(1, -1), ln_ffn_b[i].reshape(1, -1))
    return xf.reshape(batch, seq, d)
```
